```python
import functools
import jax, jax.numpy as jnp
from jax import lax
import numpy as np


D_MODEL = 1024
BATCH = 16
SEQ = 256
DEPTH = 4
DEC_BATCH = 8
DEC_SEQ = 4096
PAST_LEN = 256

GRID_W = 64
N_ATT_HEADS = 8
HEAD_DIM = 64
ATT_WIDTH = N_ATT_HEADS * HEAD_DIM
ATT_SCALE = HEAD_DIM ** -0.5
WIN_ROWS = 8
WIN_COLS = 16
LRU_WIDTH = D_MODEL // 2
LRU_BLOCKS = 8
LRU_BLOCK = LRU_WIDTH // LRU_BLOCKS
CONV_W = 4
CONV_LEFT = 2
LRU_C = 8.0
MIX_WIDTH = ATT_WIDTH + LRU_WIDTH
IN_COLS = 3 * ATT_WIDTH + 2 * LRU_WIDTH
D_FF = -(-8 * D_MODEL // (3 * 256)) * 256
EPS = 1e-6
NEG_INF = -1e30

kernel_name = 'hybrid_natten_rglru_diffusion_step'


def rmsnorm(x, g):
    xf = x.astype(jnp.float32)
    y = xf * lax.rsqrt(jnp.mean(xf * xf, axis=-1, keepdims=True) + EPS)
    return (y * g.astype(jnp.float32)).astype(x.dtype)


def adaln(cvec, w_mod, b_mod):
    m = jax.nn.silu(cvec) @ w_mod + b_mod
    return [t[:, None, :] for t in jnp.split(m, 6, axis=-1)]


def split_heads(t):
    return t.reshape(t.shape[0], t.shape[1], N_ATT_HEADS, HEAD_DIM)


def context_attention(q, k, v):
    s = jnp.einsum('bqhd,bkhd->bhqk', q, k).astype(jnp.float32) * ATT_SCALE
    p = jax.nn.softmax(s, axis=-1).astype(v.dtype)
    o = jnp.einsum('bhqk,bkhd->bqhd', p, v)
    return o.reshape(o.shape[0], o.shape[1], ATT_WIDTH)


def latent_attention(q, k, v, k_ctx, v_ctx, rpb):
    bsz, t_len = q.shape[0], q.shape[1]
    rows = t_len // GRID_W
    kh = min(WIN_ROWS, rows)
    n_loc = kh * GRID_W
    qg = q.reshape(bsz, rows, GRID_W, N_ATT_HEADS, HEAD_DIM)
    kg = k.reshape(bsz, rows, GRID_W, N_ATT_HEADS, HEAD_DIM)
    vg = v.reshape(bsz, rows, GRID_W, N_ATT_HEADS, HEAD_DIM)
    col = jnp.arange(GRID_W)
    cs = jnp.clip(col - WIN_COLS // 2, 0, GRID_W - WIN_COLS)
    col_ok = (col[None, :] >= cs[:, None]) & (col[None, :] < cs[:, None] + WIN_COLS)
    col_idx = jnp.clip(col[None, :] - col[:, None] + WIN_COLS - 1, 0, 2 * WIN_COLS - 2)
    key_ok = jnp.tile(col_ok, (1, kh))

    def row_block(r):
        rs = jnp.clip(r - kh // 2, 0, rows - kh)
        kb = lax.dynamic_slice_in_dim(kg, rs, kh, axis=1).reshape(bsz, n_loc, N_ATT_HEADS, HEAD_DIM)
        vb = lax.dynamic_slice_in_dim(vg, rs, kh, axis=1).reshape(bsz, n_loc, N_ATT_HEADS, HEAD_DIM)
        qr = lax.dynamic_index_in_dim(qg, r, axis=1, keepdims=False)
        row_idx = rs + jnp.arange(kh) - r + WIN_ROWS - 1
        bias = rpb[:, row_idx[:, None, None], col_idx[None, :, :]]
        bias = bias.transpose(0, 2, 1, 3).reshape(N_ATT_HEADS, GRID_W, n_loc).astype(jnp.float32)
        s_loc = jnp.einsum('bqhd,bkhd->bhqk', qr, kb).astype(jnp.float32) * ATT_SCALE + bias
        s_loc = jnp.where(key_ok, s_loc, NEG_INF)
        s_ctx = jnp.einsum('bqhd,bkhd->bhqk', qr, k_ctx).astype(jnp.float32) * ATT_SCALE
        p = jax.nn.softmax(jnp.concatenate([s_loc, s_ctx], axis=-1), axis=-1).astype(v.dtype)
        return (jnp.einsum('bhqk,bkhd->bqhd', p[..., :n_loc], vb)
                + jnp.einsum('bhqk,bkhd->bqhd', p[..., n_loc:], v_ctx))

    o = lax.map(row_block, jnp.arange(rows))
    return o.transpose(1, 0, 2, 3, 4).reshape(bsz, t_len, ATT_WIDTH)


def centred_conv(x, w, b):
    t_len = x.shape[1]
    xp = jnp.pad(x, ((0, 0), (CONV_LEFT, CONV_W - 1 - CONV_LEFT), (0, 0)))
    y = b
    for j in range(CONV_W):
        y = y + w[j] * xp[:, j:j + t_len]
    return y


def block_diag(x, w, b):
    xb = x.reshape(x.shape[0], x.shape[1], LRU_BLOCKS, LRU_BLOCK)
    return jnp.einsum('btnj,njk->btnk', xb, w).reshape(x.shape) + b


def _lru_combine(e1, e2):
    a1, b1 = e1
    a2, b2 = e2
    return a1 * a2, a2 * b1 + b2


def rg_lru(xc, a_param, w_r, b_r, w_i, b_i, h0, reverse):
    r = jax.nn.sigmoid(block_diag(xc, w_r, b_r).astype(jnp.float32))
    i = jax.nn.sigmoid(block_diag(xc, w_i, b_i).astype(jnp.float32))
    log_a = LRU_C * r * jax.nn.log_sigmoid(a_param.astype(jnp.float32))
    a = jnp.exp(log_a)
    bx = jnp.sqrt(-jnp.expm1(2.0 * log_a)) * i * xc.astype(jnp.float32)
    edge = -1 if reverse else 0
    bx = bx.at[:, edge].add(a[:, edge] * h0.astype(jnp.float32))
    _, h = lax.associative_scan(_lru_combine, (a, bx), reverse=reverse, axis=1)
    return h


def lru_branch(xb, yb, conv_w, conv_b, lru_a, lru_wr, lru_br, lru_wi, lru_bi, h0):
    xc = centred_conv(xb, conv_w, conv_b)
    hf = rg_lru(xc, lru_a[0], lru_wr[0], lru_br[0], lru_wi[0], lru_bi[0], h0[:, 0], False)
    hb = rg_lru(xc, lru_a[1], lru_wr[1], lru_br[1], lru_wi[1], lru_bi[1], h0[:, 1], True)
    out = (hf + hb).astype(yb.dtype) * jax.nn.gelu(yb)
    final = jnp.stack([hf[:, -1], hb[:, 0]], axis=1)
    return out, final


def trunk_layer(x, mod, attend, h0, norm1, norm2, w_in, w_out, conv_w, conv_b,
                lru_a, lru_wr, lru_br, lru_wi, lru_bi, w_gate, w_up, w_down):
    sh1, sc1, g1, sh2, sc2, g2 = mod
    h = rmsnorm(x, norm1) * (1 + sc1) + sh1
    p = h @ w_in
    q = split_heads(p[..., :ATT_WIDTH])
    k = split_heads(p[..., ATT_WIDTH:2 * ATT_WIDTH])
    v = split_heads(p[..., 2 * ATT_WIDTH:3 * ATT_WIDTH])
    xb = p[..., 3 * ATT_WIDTH:3 * ATT_WIDTH + LRU_WIDTH]
    yb = p[..., 3 * ATT_WIDTH + LRU_WIDTH:]
    att = attend(q, k, v)
    rec, h_final = lru_branch(xb, yb, conv_w, conv_b, lru_a, lru_wr, lru_br, lru_wi, lru_bi, h0)
    x = x + g1 * (jnp.concatenate([att, rec], axis=-1) @ w_out)
    h2 = rmsnorm(x, norm2) * (1 + sc2) + sh2
    x = x + g2 * ((jax.nn.silu(h2 @ w_gate) * (h2 @ w_up)) @ w_down)
    return x, k, v, h_final


def setup_inputs(seed: int = 0) -> dict:
    key = jax.random.key(seed)
    ks = jax.random.split(key, 32)

    def nrm(k, shape, scale):
        return jax.random.normal(k, shape, jnp.float32) * scale

    u = jax.random.uniform(ks[16], (DEPTH, 2, LRU_WIDTH), jnp.float32, 0.9, 0.999)
    return {
        'x_prompt': nrm(ks[0], (BATCH, SEQ, D_MODEL), 1.0),
        'x_sample': nrm(ks[1], (DEC_BATCH, DEC_SEQ, D_MODEL), 1.0),
        'cache_k': nrm(ks[2], (DEC_BATCH, DEPTH, PAST_LEN, N_ATT_HEADS, HEAD_DIM), 1.0),
        'cache_v': nrm(ks[3], (DEC_BATCH, DEPTH, PAST_LEN, N_ATT_HEADS, HEAD_DIM), 1.0),
        'state_lru': nrm(ks[4], (DEC_BATCH, DEPTH, 2, LRU_WIDTH), 0.5),
        'c': nrm(ks[5], (DEC_BATCH, D_MODEL), 1.0),
        'c_ctx': nrm(ks[6], (D_MODEL,), 1.0),
        'w_mod': nrm(ks[7], (DEPTH, D_MODEL, 6 * D_MODEL), 0.5 * D_MODEL ** -0.5),
        'b_mod': nrm(ks[8], (DEPTH, 6 * D_MODEL), 0.02),
        'norm1': 1.0 + nrm(ks[9], (DEPTH, D_MODEL), 0.02),
        'norm2': 1.0 + nrm(ks[10], (DEPTH, D_MODEL), 0.02),
        'w_in': nrm(ks[11], (DEPTH, D_MODEL, IN_COLS), D_MODEL ** -0.5),
        'w_out': nrm(ks[12], (DEPTH, MIX_WIDTH, D_MODEL), MIX_WIDTH ** -0.5),
        'rpb': nrm(ks[13], (DEPTH, N_ATT_HEADS, 2 * WIN_ROWS - 1, 2 * WIN_COLS - 1), 0.1),
        'conv_w': nrm(ks[14], (DEPTH, CONV_W, LRU_WIDTH), CONV_W ** -0.5),
        'conv_b': nrm(ks[15], (DEPTH, LRU_WIDTH), 0.02),
        'lru_a': jnp.log(u) - jnp.log1p(-u),
        'lru_wr': nrm(ks[17], (DEPTH, 2, LRU_BLOCKS, LRU_BLOCK, LRU_BLOCK), LRU_BLOCK ** -0.5),
        'lru_br': nrm(ks[18], (DEPTH, 2, LRU_WIDTH), 0.02),
        'lru_wi': nrm(ks[19], (DEPTH, 2, LRU_BLOCKS, LRU_BLOCK, LRU_BLOCK), LRU_BLOCK ** -0.5),
        'lru_bi': nrm(ks[20], (DEPTH, 2, LRU_WIDTH), 0.02),
        'w_gate': nrm(ks[21], (DEPTH, D_MODEL, D_FF), D_MODEL ** -0.5),
        'w_up': nrm(ks[22], (DEPTH, D_MODEL, D_FF), D_MODEL ** -0.5),
        'w_down': nrm(ks[23], (DEPTH, D_FF, D_MODEL), D_FF ** -0.5),
        'norm_final': 1.0 + nrm(ks[24], (D_MODEL,), 0.02),
    }


def reference(x_prompt, x_sample, cache_k, cache_v, state_lru, c, c_ctx, w_mod, b_mod, norm1, norm2,
              w_in, w_out, rpb, conv_w, conv_b, lru_a, lru_wr, lru_br, lru_wi, lru_bi,
              w_gate, w_up, w_down, norm_final):
    xp, xs = x_prompt, x_sample
    h0_ctx = jnp.zeros((x_prompt.shape[0], 2, LRU_WIDTH), jnp.float32)
    ks_out, vs_out, hs_out = [], [], []
    for l in range(DEPTH):
        weights = (norm1[l], norm2[l], w_in[l], w_out[l], conv_w[l], conv_b[l], lru_a[l], lru_wr[l],
                   lru_br[l], lru_wi[l], lru_bi[l], w_gate[l], w_up[l], w_down[l])
        mod_ctx = adaln(c_ctx[None, :], w_mod[l], b_mod[l])
        xp, k_l, v_l, h_l = trunk_layer(xp, mod_ctx, context_attention, h0_ctx, *weights)
        ks_out.append(k_l)
        vs_out.append(v_l)
        hs_out.append(h_l)
        mod_lat = adaln(c, w_mod[l], b_mod[l])
        attend_lat = functools.partial(latent_attention, k_ctx=cache_k[:, l], v_ctx=cache_v[:, l], rpb=rpb[l])
        xs, _, _, _ = trunk_layer(xs, mod_lat, attend_lat, state_lru[:, l], *weights)
    y_prompt = rmsnorm(xp, norm_final)
    y_sample = rmsnorm(xs, norm_final)
    new_cache_k = jnp.stack(ks_out, axis=1)
    new_cache_v = jnp.stack(vs_out, axis=1)
    new_state_lru = jnp.stack(hs_out, axis=1)
    return (y_prompt, y_sample, new_cache_k, new_cache_v, new_state_lru)
```

```python
import functools

import jax
import jax.numpy as jnp
from jax import lax
from jax.experimental import pallas as pl
from jax.experimental.pallas import tpu as pltpu

F32 = jnp.float32
BF16 = jnp.bfloat16

D_MODEL = 1024
GRID_W = 64
N_ATT_HEADS = 8
HEAD_DIM = 64
ATT_WIDTH = N_ATT_HEADS * HEAD_DIM
ATT_SCALE = HEAD_DIM ** -0.5
WIN_ROWS = 8
WIN_COLS = 16
LRU_WIDTH = D_MODEL // 2
LRU_BLOCKS = 8
LRU_BLOCK = LRU_WIDTH // LRU_BLOCKS
CONV_W = 4
CONV_LEFT = 2
LRU_C = 8.0
IN_COLS = 3 * ATT_WIDTH + 2 * LRU_WIDTH
EPS = 1e-6
NEG_INF = -1e30

LANES = 128
SUBLANES = 8
HEAD_PAIRS = ATT_WIDTH // LANES
N_ROW_IDX = 2 * WIN_ROWS - 1
N_COL_IDX = 2 * WIN_COLS - 1
N_BIAS_TILES = N_ROW_IDX - 1
MOD_ROWS = 16
MIB = 1024 * 1024


def _cparams(semantics, vmem_mib):
    return pltpu.CompilerParams(dimension_semantics=semantics, vmem_limit_bytes=vmem_mib * MIB)


def _resident(shape):
    zeros = (0,) * len(shape)
    return pl.BlockSpec(shape, lambda *_: zeros, pipeline_mode=pl.Buffered(1))


def _rmsnorm(x, g):
    return (x * lax.rsqrt(jnp.mean(x * x, axis=-1, keepdims=True) + EPS)) * g


def _mod_kernel(c_ref, w_ref, b_ref, o_ref):
    cv = c_ref[...]
    s = (cv * jax.nn.sigmoid(cv)).astype(BF16)
    o_ref[0] = jnp.dot(s, w_ref[0].astype(BF16), preferred_element_type=F32) + b_ref[0]


def _modulation(cvec, w_mod, b_mod):
    depth, d, n = w_mod.shape
    tn = n // 4
    return pl.pallas_call(
        _mod_kernel,
        grid=(depth, n // tn),
        in_specs=[
            pl.BlockSpec((MOD_ROWS, d), lambda l, j: (0, 0)),
            pl.BlockSpec((1, d, tn), lambda l, j: (l, 0, j)),
            pl.BlockSpec((1, 1, tn), lambda l, j: (l, 0, j)),
        ],
        out_specs=pl.BlockSpec((1, MOD_ROWS, tn), lambda l, j: (l, 0, j)),
        out_shape=jax.ShapeDtypeStruct((depth, MOD_ROWS, n), F32),
        compiler_params=_cparams(("arbitrary", "arbitrary"), 40),
        name="adaln_mod",
    )(cvec, w_mod, b_mod.reshape(depth, 1, n))


def _bias_table_kernel(rpb_ref, o_ref):
    lh = pl.program_id(0)
    shape = (GRID_W, LANES)
    c = lax.broadcasted_iota(jnp.int32, shape, 0)
    lane = lax.broadcasted_iota(jnp.int32, shape, 1)
    upper = lane >= GRID_W
    kc = jnp.where(upper, lane - GRID_W, lane)
    cs = jnp.clip(c - WIN_COLS // 2, 0, GRID_W - WIN_COLS)
    ok = (kc >= cs) & (kc < cs + WIN_COLS)
    d = kc - c + WIN_COLS - 1

    def tile(i, carry):
        val = jnp.full(shape, NEG_INF, F32)
        for j in range(N_COL_IDX):
            lo = rpb_ref[lh, i * N_COL_IDX + j]
            hi = rpb_ref[lh, (i + 1) * N_COL_IDX + j]
            val = jnp.where(ok & (d == j), jnp.where(upper, hi, lo), val)
        o_ref[0, i] = val
        return carry

    lax.fori_loop(0, N_BIAS_TILES, tile, 0)


def _bias_tables(rpb):
    depth = rpb.shape[0]
    n = depth * N_ATT_HEADS
    out = pl.pallas_call(
        _bias_table_kernel,
        grid=(n,),
        in_specs=[pl.BlockSpec(memory_space=pltpu.SMEM)],
        out_specs=pl.BlockSpec((1, N_BIAS_TILES, GRID_W, LANES), lambda i: (i, 0, 0, 0)),
        out_shape=jax.ShapeDtypeStruct((n, N_BIAS_TILES, GRID_W, LANES), F32),
        compiler_params=_cparams(("arbitrary",), 16),
        name="bias_tables",
    )(rpb.reshape(n, N_ROW_IDX * N_COL_IDX))
    return out.reshape(depth, HEAD_PAIRS, 2, N_BIAS_TILES, GRID_W, LANES)


def _in_proj_kernel(x_ref, mod_ref, g_ref, w_ref, q_ref, k_ref, v_ref, xb_ref, yb_ref, *kv_f32_refs):
    h = _rmsnorm(x_ref[0], g_ref[...]) * (1.0 + mod_ref[0, 1:2, :]) + mod_ref[0, 0:1, :]
    p = jnp.dot(h.astype(BF16), w_ref[...], preferred_element_type=F32)
    for j in range(HEAD_PAIRS):
        lo = j * LANES
        q_ref[0, j] = (p[:, lo:lo + LANES] * ATT_SCALE).astype(BF16)
        k_ref[0, j] = p[:, ATT_WIDTH + lo:ATT_WIDTH + lo + LANES].astype(BF16)
        v_ref[0, j] = p[:, 2 * ATT_WIDTH + lo:2 * ATT_WIDTH + lo + LANES].astype(BF16)
    xb_ref[0] = p[:, 3 * ATT_WIDTH:3 * ATT_WIDTH + LRU_WIDTH]
    yb_ref[0] = p[:, 3 * ATT_WIDTH + LRU_WIDTH:]
    if kv_f32_refs:
        kf_ref, vf_ref = kv_f32_refs
        kf_ref[0] = p[:, ATT_WIDTH:2 * ATT_WIDTH]
        vf_ref[0] = p[:, 2 * ATT_WIDTH:3 * ATT_WIDTH]


def _in_proj(x, mod, g, w, *, emit_kv_f32):
    b, t, d = x.shape
    tm = min(512, t)
    shared = mod.shape[0] == 1
    mod_map = (lambda i, j: (0, 0, 0)) if shared else (lambda i, j: (i, 0, 0))
    heads_spec = pl.BlockSpec((1, HEAD_PAIRS, tm, LANES), lambda i, j: (i, 0, j, 0))
    heads_shape = jax.ShapeDtypeStruct((b, HEAD_PAIRS, t, LANES), BF16)
    half_spec = pl.BlockSpec((1, tm, LRU_WIDTH), lambda i, j: (i, j, 0))
    half_shape = jax.ShapeDtypeStruct((b, t, LRU_WIDTH), F32)
    n_f32 = 4 if emit_kv_f32 else 2
    return pl.pallas_call(
        _in_proj_kernel,
        grid=(b, t // tm),
        in_specs=[
            pl.BlockSpec((1, tm, d), lambda i, j: (i, j, 0)),
            pl.BlockSpec((1, 6, d), mod_map),
            _resident((1, d)),
            _resident((d, IN_COLS)),
        ],
        out_specs=[heads_spec] * 3 + [half_spec] * n_f32,
        out_shape=[heads_shape] * 3 + [half_shape] * n_f32,
        compiler_params=_cparams(("parallel", "parallel"), 48),
        name="in_proj",
    )(x, mod, g, w)


def _head_masks(shape):
    lane = lax.broadcasted_iota(jnp.int32, shape, 1)
    return lane < HEAD_DIM, lane >= HEAD_DIM


def _scores(qh, k):
    return lax.dot_general(qh, k, (((1,), (1,)), ((), ())), preferred_element_type=F32)


def _ctx_attn_kernel(q_ref, k_ref, v_ref, o_ref):
    q, k, v = q_ref[0, 0], k_ref[0, 0], v_ref[0, 0]
    outs = []
    for hm in _head_masks(q.shape):
        s = _scores(jnp.where(hm, q, jnp.zeros_like(q)), k)
        e = jnp.exp(s - jnp.max(s, axis=-1, keepdims=True))
        l = jnp.sum(e, axis=-1, keepdims=True)
        outs.append(jnp.dot(e.astype(BF16), v, preferred_element_type=F32) / l)
    lo, _ = _head_masks(outs[0].shape)
    o_ref[0] = jnp.where(lo, outs[0], outs[1]).astype(BF16)


def _ctx_attention(q, k, v):
    b, _, t, _ = q.shape
    spec = pl.BlockSpec((1, 1, t, LANES), lambda i, j: (i, j, 0, 0))
    return pl.pallas_call(
        _ctx_attn_kernel,
        grid=(b, HEAD_PAIRS),
        in_specs=[spec] * 3,
        out_specs=pl.BlockSpec((1, t, LANES), lambda i, j: (i, 0, j)),
        out_shape=jax.ShapeDtypeStruct((b, t, ATT_WIDTH), BF16),
        compiler_params=_cparams(("parallel", "parallel"), 32),
        name="ctx_attention",
    )(q, k, v)


def _lat_attn_kernel(q_ref, k_ref, v_ref, kc_ref, vc_ref, t2_ref, o_ref, *, rows_per_step, n_rows):
    rb = pl.program_id(2)
    kc, vc = kc_ref[0, 0], vc_ref[0, 0]
    n_loc = WIN_ROWS * GRID_W
    masks = _head_masks((GRID_W, LANES))

    def row_body(i, carry):
        r = rb * rows_per_step + i
        rs = jnp.clip(r - WIN_ROWS // 2, 0, n_rows - WIN_ROWS)
        off = rs - r + WIN_ROWS - 1
        q = q_ref[0, 0, pl.ds(pl.multiple_of(i * GRID_W, GRID_W), GRID_W), :]
        k_start = pl.multiple_of(rs * GRID_W, GRID_W)
        kb = k_ref[0, 0, pl.ds(k_start, n_loc), :]
        vb = v_ref[0, 0, pl.ds(k_start, n_loc), :]
        outs = []
        for h, hm in enumerate(masks):
            qh = jnp.where(hm, q, jnp.zeros_like(q))
            bias = jnp.concatenate([t2_ref[0, h, off + 2 * j] for j in range(WIN_ROWS // 2)], axis=1)
            s_loc = _scores(qh, kb) + bias
            s_ctx = _scores(qh, kc)
            m = jnp.maximum(jnp.max(s_loc, axis=-1, keepdims=True), jnp.max(s_ctx, axis=-1, keepdims=True))
            e_loc = jnp.exp(s_loc - m)
            e_ctx = jnp.exp(s_ctx - m)
            l = jnp.sum(e_loc, axis=-1, keepdims=True) + jnp.sum(e_ctx, axis=-1, keepdims=True)
            o = (jnp.dot(e_loc.astype(BF16), vb, preferred_element_type=F32)
                 + jnp.dot(e_ctx.astype(BF16), vc, preferred_element_type=F32))
            outs.append(o / l)
        o_ref[0, pl.ds(pl.multiple_of(i * GRID_W, GRID_W), GRID_W), :] = (
            jnp.where(masks[0], outs[0], outs[1]).astype(BF16))
        return carry

    lax.fori_loop(0, rows_per_step, row_body, 0)


def _lat_attention(q, k, v, kc, vc, t2):
    b, _, t, _ = q.shape
    lc = kc.shape[2]
    n_rows = t // GRID_W
    rows_per_step = 8
    tq = rows_per_step * GRID_W
    full = pl.BlockSpec((1, 1, t, LANES), lambda i, j, r: (i, j, 0, 0))
    ctx = pl.BlockSpec((1, 1, lc, LANES), lambda i, j, r: (i, j, 0, 0))
    return pl.pallas_call(
        functools.partial(_lat_attn_kernel, rows_per_step=rows_per_step, n_rows=n_rows),
        grid=(b, HEAD_PAIRS, n_rows // rows_per_step),
        in_specs=[
            pl.BlockSpec((1, 1, tq, LANES), lambda i, j, r: (i, j, r, 0)),
            full, full, ctx, ctx,
            pl.BlockSpec((1, 2, N_BIAS_TILES, GRID_W, LANES), lambda i, j, r: (j, 0, 0, 0, 0)),
        ],
        out_specs=pl.BlockSpec((1, tq, LANES), lambda i, j, r: (i, r, j)),
        out_shape=jax.ShapeDtypeStruct((b, t, ATT_WIDTH), BF16),
        compiler_params=_cparams(("parallel", "parallel", "arbitrary"), 32),
        name="lat_attention",
    )(q, k, v, kc, vc, t2)


def _log_sigmoid(x):
    return jnp.minimum(x, 0.0) - jnp.log1p(jnp.exp(-jnp.abs(x)))


def _lru_kernel(xb_ref, cw_ref, cb_ref, wg_ref, bg_ref, la_ref, h0_ref, *rest, reverse, tc, t_len):
    if reverse:
        out_ref, hfin_ref, a_s, b_s, carry_s = rest
    else:
        hb_ref, yb_ref, out_ref, hfin_ref, a_s, b_s, carry_s = rest
    j = pl.program_id(1)
    n = pl.num_programs(1)
    cj = (n - 1 - j) if reverse else j
    t0 = pl.multiple_of(cj * tc, tc)

    @pl.when(j == 0)
    def _():
        carry_s[...] = jnp.broadcast_to(h0_ref[0], carry_s.shape)

    x = xb_ref[0, pl.ds(t0, tc), :]
    prev = xb_ref[0, pl.ds(pl.multiple_of(jnp.maximum(t0 - SUBLANES, 0), SUBLANES), SUBLANES), :]
    nxt = xb_ref[0, pl.ds(pl.multiple_of(jnp.minimum(t0 + tc, t_len - SUBLANES), SUBLANES), SUBLANES), :]
    prev = jnp.where(t0 > 0, prev, 0.0)
    nxt = jnp.where(t0 + tc < t_len, nxt, 0.0)
    xe = jnp.concatenate([prev, x, nxt], axis=0)
    ne = tc + 2 * SUBLANES

    def shifted(offset):
        return pltpu.roll(xe, (-offset) % ne, 0)[SUBLANES:SUBLANES + tc]

    xc = cb_ref[...] + cw_ref[0:1, :] * shifted(-2)
    xc = xc + cw_ref[1:2, :] * shifted(-1)
    xc = xc + cw_ref[2:3, :] * x
    xc = xc + cw_ref[3:4, :] * shifted(1)

    xcb = xc.astype(BF16)
    half = LRU_WIDTH // 2
    pre = [jnp.dot(xcb[:, hf * half:(hf + 1) * half], wg_ref[hf], preferred_element_type=F32) for hf in range(2)]
    r = jax.nn.sigmoid(jnp.concatenate([pre[0][:, :half], pre[1][:, :half]], axis=1) + bg_ref[0:1, :])
    ig = jax.nn.sigmoid(jnp.concatenate([pre[0][:, half:], pre[1][:, half:]], axis=1) + bg_ref[1:2, :])
    log_a = (LRU_C * r) * _log_sigmoid(la_ref[...])
    a = jnp.exp(log_a)
    a_s[...] = a
    b_s[...] = jnp.sqrt(-jnp.tanh(log_a) * (1.0 + a * a)) * ig * xc

    row = lax.broadcasted_iota(jnp.int32, (SUBLANES, LRU_WIDTH), 0)
    n_tiles = tc // SUBLANES

    def tile_body(g, carry):
        gt = (n_tiles - 1 - g) if reverse else g
        rows = pl.ds(pl.multiple_of(gt * SUBLANES, SUBLANES), SUBLANES)
        a = a_s[rows, :]
        bv = b_s[rows, :]
        for dist in (1, 2, 4):
            if reverse:
                shift, valid = SUBLANES - dist, row < SUBLANES - dist
            else:
                shift, valid = dist, row >= dist
            a_n = pltpu.roll(a, shift, 0)
            b_n = pltpu.roll(bv, shift, 0)
            bv = jnp.where(valid, a * b_n + bv, bv)
            a = jnp.where(valid, a * a_n, a)
        h = bv + a * carry
        b_s[rows, :] = h
        edge = h[0:1, :] if reverse else h[SUBLANES - 1:SUBLANES, :]
        return jnp.broadcast_to(edge, carry.shape)

    carry_s[...] = lax.fori_loop(0, n_tiles, tile_body, carry_s[...])

    if reverse:
        out_ref[0] = b_s[...]
    else:
        out_ref[0] = ((b_s[...] + hb_ref[0]) * jax.nn.gelu(yb_ref[0])).astype(BF16)

    @pl.when(j == n - 1)
    def _():
        hfin_ref[0] = carry_s[0:1, :]


def _lru_pass(xb, conv_w, conv_b, wg, bg, la, h0, hb=None, yb=None):
    b, t, c = xb.shape
    reverse = hb is None
    tc = min(512, t)
    n = t // tc
    chunk_map = (lambda i, j: (i, n - 1 - j, 0)) if reverse else (lambda i, j: (i, j, 0))
    chunk = pl.BlockSpec((1, tc, c), chunk_map)
    in_specs = [
        pl.BlockSpec((1, t, c), lambda i, j: (i, 0, 0)),
        _resident((CONV_W, c)),
        _resident((1, c)),
        _resident((2, c // 2, c)),
        _resident((2, c)),
        _resident((1, c)),
        pl.BlockSpec((1, 1, c), lambda i, j: (i, 0, 0)),
    ]
    args = [xb, conv_w, conv_b, wg, bg, la, h0]
    if not reverse:
        in_specs += [chunk, chunk]
        args += [hb, yb]
    return pl.pallas_call(
        functools.partial(_lru_kernel, reverse=reverse, tc=tc, t_len=t),
        grid=(b, n),
        in_specs=in_specs,
        out_specs=[chunk, pl.BlockSpec((1, 1, c), lambda i, j: (i, 0, 0))],
        out_shape=[jax.ShapeDtypeStruct((b, t, c), F32 if reverse else BF16),
                   jax.ShapeDtypeStruct((b, 1, c), F32)],
        scratch_shapes=[pltpu.VMEM((tc, c), F32), pltpu.VMEM((tc, c), F32), pltpu.VMEM((SUBLANES, c), F32)],
        compiler_params=_cparams(("parallel", "arbitrary"), 48),
        name="lru_bwd" if reverse else "lru_fwd",
    )(*args)


def _out_ffn_kernel(x_ref, att_ref, rec_ref, mod_ref, g_ref, wo_ref, wg_ref, wu_ref, wd_ref, *rest, final_norm):
    o_ref = rest[-1]
    mix = jnp.concatenate([att_ref[0], rec_ref[0]], axis=1)
    x1 = x_ref[0] + mod_ref[0, 2:3, :] * jnp.dot(mix, wo_ref[...], preferred_element_type=F32)
    h2 = (_rmsnorm(x1, g_ref[...]) * (1.0 + mod_ref[0, 4:5, :]) + mod_ref[0, 3:4, :]).astype(BF16)
    gate = jnp.dot(h2, wg_ref[...], preferred_element_type=F32)
    up = jnp.dot(h2, wu_ref[...], preferred_element_type=F32)
    act = ((gate * jax.nn.sigmoid(gate)) * up).astype(BF16)
    x2 = x1 + mod_ref[0, 5:6, :] * jnp.dot(act, wd_ref[...], preferred_element_type=F32)
    if final_norm:
        x2 = _rmsnorm(x2, rest[0][...])
    o_ref[0] = x2


def _out_ffn(x, att, rec, mod, g2, w_out, w_gate, w_up, w_down, g_final=None):
    b, t, d = x.shape
    d_ff = w_gate.shape[1]
    tm = min(256, t)
    shared = mod.shape[0] == 1
    mod_map = (lambda i, j: (0, 0, 0)) if shared else (lambda i, j: (i, 0, 0))
    tok = lambda w: pl.BlockSpec((1, tm, w), lambda i, j: (i, j, 0))
    in_specs = [
        tok(d), tok(ATT_WIDTH), tok(LRU_WIDTH),
        pl.BlockSpec((1, 6, d), mod_map),
        _resident((1, d)),
        _resident((d, d)),
        _resident((d, d_ff)),
        _resident((d, d_ff)),
        _resident((d_ff, d)),
    ]
    args = [x, att, rec, mod, g2, w_out, w_gate, w_up, w_down]
    if g_final is not None:
        in_specs.append(_resident((1, d)))
        args.append(g_final)
    return pl.pallas_call(
        functools.partial(_out_ffn_kernel, final_norm=g_final is not None),
        grid=(b, t // tm),
        in_specs=in_specs,
        out_specs=tok(d),
        out_shape=jax.ShapeDtypeStruct((b, t, d), F32),
        compiler_params=_cparams(("parallel", "parallel"), 56),
        name="out_ffn",
    )(*args)


def _block_diag_halves(w):
    per_half = LRU_BLOCKS // 2
    w4 = w.reshape(2, per_half, LRU_BLOCK, LRU_BLOCK)
    eye = jnp.eye(per_half, dtype=w.dtype)
    return jnp.einsum('hnjk,nm->hnjmk', w4, eye).reshape(2, per_half * LRU_BLOCK, per_half * LRU_BLOCK)


def _head_pair_layout(cache):
    b, depth, p = cache.shape[:3]
    return cache.reshape(b, depth, p, HEAD_PAIRS, LANES).transpose(1, 0, 3, 2, 4).astype(BF16)


def kernel(x_prompt, x_sample, cache_k, cache_v, state_lru, c, c_ctx, w_mod, b_mod, norm1, norm2, w_in, w_out, rpb, conv_w, conv_b, lru_a, lru_wr, lru_br, lru_wi, lru_bi, w_gate, w_up, w_down, norm_final):
    depth = w_in.shape[0]
    n_ctx, n_lat = x_prompt.shape[0], x_sample.shape[0]
    d = x_prompt.shape[-1]
    assert n_lat + 1 <= MOD_ROWS

    cvec = jnp.zeros((MOD_ROWS, d), F32).at[:n_lat].set(c).at[n_lat].set(c_ctx)
    mod = _modulation(cvec, w_mod, b_mod).reshape(depth, MOD_ROWS, 6, d)
    t2 = _bias_tables(rpb)
    ck, cv = _head_pair_layout(cache_k), _head_pair_layout(cache_v)

    w_in_b, w_out_b = w_in.astype(BF16), w_out.astype(BF16)
    w_gate_b, w_up_b, w_down_b = w_gate.astype(BF16), w_up.astype(BF16), w_down.astype(BF16)
    wg = jnp.stack([
        jnp.stack([jnp.concatenate([_block_diag_halves(lru_wr[l, dr]), _block_diag_halves(lru_wi[l, dr])], axis=-1)
                   for dr in range(2)]) for l in range(depth)]).astype(BF16)
    bg = jnp.stack([lru_br, lru_bi], axis=2)
    h0_ctx = jnp.zeros((n_ctx, 1, LRU_WIDTH), F32)

    def layer(x, l, mod_l, attend, h0_f, h0_b, last, emit_kv_f32):
        outs = _in_proj(x, mod_l, norm1[l][None], w_in_b[l], emit_kv_f32=emit_kv_f32)
        q, k, v, xb, yb = outs[:5]
        att = attend(q, k, v)
        lru_args = (xb, conv_w[l], conv_b[l][None])
        hb, hb_fin = _lru_pass(*lru_args, wg[l, 1], bg[l, 1], lru_a[l, 1][None], h0_b)
        rec, hf_fin = _lru_pass(*lru_args, wg[l, 0], bg[l, 0], lru_a[l, 0][None], h0_f, hb=hb, yb=yb)
        x = _out_ffn(x, att, rec, mod_l, norm2[l][None], w_out_b[l], w_gate_b[l], w_up_b[l], w_down_b[l],
                     g_final=norm_final[None] if last else None)
        return x, outs[5:], jnp.concatenate([hf_fin, hb_fin], axis=1)

    xp, xs = x_prompt, x_sample
    ks_out, vs_out, hs_out = [], [], []
    for l in range(depth):
        last = l == depth - 1
        xp, (k_l, v_l), h_l = layer(xp, l, mod[l, n_lat:n_lat + 1], _ctx_attention, h0_ctx, h0_ctx, last, True)
        ks_out.append(k_l.reshape(n_ctx, -1, N_ATT_HEADS, HEAD_DIM))
        vs_out.append(v_l.reshape(n_ctx, -1, N_ATT_HEADS, HEAD_DIM))
        hs_out.append(h_l)
        attend_lat = functools.partial(_lat_attention, kc=ck[l], vc=cv[l], t2=t2[l])
        xs, _, _ = layer(xs, l, mod[l, :n_lat], attend_lat, state_lru[:, l, 0:1], state_lru[:, l, 1:2], last, False)
    return (xp, xs, jnp.stack(ks_out, axis=1), jnp.stack(vs_out, axis=1), jnp.stack(hs_out, axis=1))
```

```python
import functools

import jax
import jax.numpy as jnp
from jax import lax
from jax.experimental import pallas as pl
from jax.experimental.pallas import tpu as pltpu

F32 = jnp.float32
BF16 = jnp.bfloat16

D_MODEL = 1024
GRID_W = 64
N_ATT_HEADS = 8
HEAD_DIM = 64
ATT_WIDTH = N_ATT_HEADS * HEAD_DIM
ATT_SCALE = HEAD_DIM ** -0.5
WIN_ROWS = 8
WIN_COLS = 16
LRU_WIDTH = D_MODEL // 2
LRU_BLOCKS = 8
LRU_BLOCK = LRU_WIDTH // LRU_BLOCKS
CONV_W = 4
CONV_LEFT = 2
LRU_C = 8.0
IN_COLS = 3 * ATT_WIDTH + 2 * LRU_WIDTH
EPS = 1e-6
NEG_INF = -1e30

LANES = 128
SUBLANES = 8
HEAD_PAIRS = ATT_WIDTH // LANES
N_ROW_IDX = 2 * WIN_ROWS - 1
N_COL_IDX = 2 * WIN_COLS - 1
N_BIAS_TILES = N_ROW_IDX - 1
MOD_ROWS = 16
MIB = 1024 * 1024


def _cparams(semantics, vmem_mib):
    return pltpu.CompilerParams(dimension_semantics=semantics, vmem_limit_bytes=vmem_mib * MIB)


def _resident(shape):
    zeros = (0,) * len(shape)
    return pl.BlockSpec(shape, lambda *_: zeros, pipeline_mode=pl.Buffered(1))


def _rmsnorm(x, g):
    return (x * lax.rsqrt(jnp.mean(x * x, axis=-1, keepdims=True) + EPS)) * g


def _mod_kernel(c_ref, w_ref, b_ref, o_ref):
    cv = c_ref[...]
    s = (cv * jax.nn.sigmoid(cv)).astype(BF16)
    o_ref[0] = jnp.dot(s, w_ref[0].astype(BF16), preferred_element_type=F32) + b_ref[0]


def _modulation(cvec, w_mod, b_mod):
    depth, d, n = w_mod.shape
    tn = n // 4
    return pl.pallas_call(
        _mod_kernel,
        grid=(depth, n // tn),
        in_specs=[
            pl.BlockSpec((MOD_ROWS, d), lambda l, j: (0, 0)),
            pl.BlockSpec((1, d, tn), lambda l, j: (l, 0, j)),
            pl.BlockSpec((1, 1, tn), lambda l, j: (l, 0, j)),
        ],
        out_specs=pl.BlockSpec((1, MOD_ROWS, tn), lambda l, j: (l, 0, j)),
        out_shape=jax.ShapeDtypeStruct((depth, MOD_ROWS, n), F32),
        compiler_params=_cparams(("arbitrary", "arbitrary"), 40),
        name="adaln_mod",
    )(cvec, w_mod, b_mod.reshape(depth, 1, n))


def _bias_table_kernel(rpb_ref, o_ref):
    shape = (GRID_W, LANES)
    c = lax.broadcasted_iota(jnp.int32, shape, 0)
    lane = lax.broadcasted_iota(jnp.int32, shape, 1)
    upper = lane >= GRID_W
    kc = jnp.where(upper, lane - GRID_W, lane)
    cs = jnp.clip(c - WIN_COLS // 2, 0, GRID_W - WIN_COLS)
    ok = (kc >= cs) & (kc < cs + WIN_COLS)
    lo_shift = LANES - (WIN_COLS - 1)
    hi_shift = (lo_shift + GRID_W) % LANES
    for h in range(2):
        for i in range(N_BIAS_TILES):
            lo = pltpu.roll(jnp.broadcast_to(rpb_ref[0, h, i:i + 1, :], shape), lo_shift, 1, stride=1, stride_axis=0)
            hi = pltpu.roll(jnp.broadcast_to(rpb_ref[0, h, i + 1:i + 2, :], shape), hi_shift, 1, stride=1, stride_axis=0)
            o_ref[0, i, h * GRID_W:(h + 1) * GRID_W, :] = jnp.where(ok, jnp.where(upper, hi, lo), NEG_INF)


def _bias_tables(rpb):
    depth = rpb.shape[0]
    n = depth * HEAD_PAIRS
    padded = jnp.pad(rpb, ((0, 0), (0, 0), (0, 2 * SUBLANES - N_ROW_IDX), (0, LANES - N_COL_IDX)))
    out = pl.pallas_call(
        _bias_table_kernel,
        grid=(n,),
        in_specs=[pl.BlockSpec((1, 2, 2 * SUBLANES, LANES), lambda i: (i, 0, 0, 0))],
        out_specs=pl.BlockSpec((1, N_BIAS_TILES, LANES, LANES), lambda i: (i, 0, 0, 0)),
        out_shape=jax.ShapeDtypeStruct((n, N_BIAS_TILES, LANES, LANES), F32),
        compiler_params=_cparams(("arbitrary",), 16),
        name="bias_tables",
    )(padded.reshape(n, 2, 2 * SUBLANES, LANES))
    return out.reshape(depth, HEAD_PAIRS, N_BIAS_TILES, LANES, LANES)


def _in_proj_kernel(x_ref, mod_ref, g_ref, w_ref, q_ref, k_ref, v_ref, xb_ref, yb_ref, *kv_f32_refs):
    h = _rmsnorm(x_ref[0], g_ref[...]) * (1.0 + mod_ref[0, 1:2, :]) + mod_ref[0, 0:1, :]
    p = jnp.dot(h.astype(BF16), w_ref[...], preferred_element_type=F32)
    for j in range(HEAD_PAIRS):
        lo = j * LANES
        q_ref[0, j] = (p[:, lo:lo + LANES] * ATT_SCALE).astype(BF16)
        k_ref[0, j] = p[:, ATT_WIDTH + lo:ATT_WIDTH + lo + LANES].astype(BF16)
        v_ref[0, j] = p[:, 2 * ATT_WIDTH + lo:2 * ATT_WIDTH + lo + LANES].astype(BF16)
    xb_ref[0] = p[:, 3 * ATT_WIDTH:3 * ATT_WIDTH + LRU_WIDTH]
    yb_ref[0] = p[:, 3 * ATT_WIDTH + LRU_WIDTH:]
    if kv_f32_refs:
        kf_ref, vf_ref = kv_f32_refs
        kf_ref[0] = p[:, ATT_WIDTH:2 * ATT_WIDTH]
        vf_ref[0] = p[:, 2 * ATT_WIDTH:3 * ATT_WIDTH]


def _in_proj(x, mod, g, w, *, emit_kv_f32):
    b, t, d = x.shape
    tm = min(512, t)
    shared = mod.shape[0] == 1
    mod_map = (lambda i, j: (0, 0, 0)) if shared else (lambda i, j: (i, 0, 0))
    heads_spec = pl.BlockSpec((1, HEAD_PAIRS, tm, LANES), lambda i, j: (i, 0, j, 0))
    heads_shape = jax.ShapeDtypeStruct((b, HEAD_PAIRS, t, LANES), BF16)
    half_spec = pl.BlockSpec((1, tm, LRU_WIDTH), lambda i, j: (i, j, 0))
    half_shape = jax.ShapeDtypeStruct((b, t, LRU_WIDTH), F32)
    n_f32 = 4 if emit_kv_f32 else 2
    return pl.pallas_call(
        _in_proj_kernel,
        grid=(b, t // tm),
        in_specs=[
            pl.BlockSpec((1, tm, d), lambda i, j: (i, j, 0)),
            pl.BlockSpec((1, 6, d), mod_map),
            _resident((1, d)),
            _resident((d, IN_COLS)),
        ],
        out_specs=[heads_spec] * 3 + [half_spec] * n_f32,
        out_shape=[heads_shape] * 3 + [half_shape] * n_f32,
        compiler_params=_cparams(("parallel", "parallel"), 48),
        name="in_proj",
    )(x, mod, g, w)


def _head_masks(shape):
    lane = lax.broadcasted_iota(jnp.int32, shape, 1)
    return lane < HEAD_DIM, lane >= HEAD_DIM


def _scores(qh, k):
    return lax.dot_general(qh, k, (((1,), (1,)), ((), ())), preferred_element_type=F32)


def _ctx_attn_kernel(q_ref, k_ref, v_ref, o_ref):
    q, k, v = q_ref[0, 0], k_ref[0, 0], v_ref[0, 0]
    outs = []
    for hm in _head_masks(q.shape):
        s = _scores(jnp.where(hm, q, jnp.zeros_like(q)), k)
        e = jnp.exp(s - jnp.max(s, axis=-1, keepdims=True))
        l = jnp.sum(e, axis=-1, keepdims=True)
        outs.append(jnp.dot(e.astype(BF16), v, preferred_element_type=F32) / l)
    lo, _ = _head_masks(outs[0].shape)
    o_ref[0] = jnp.where(lo, outs[0], outs[1]).astype(BF16)


def _ctx_attention(q, k, v):
    b, _, t, _ = q.shape
    spec = pl.BlockSpec((1, 1, t, LANES), lambda i, j: (i, j, 0, 0))
    return pl.pallas_call(
        _ctx_attn_kernel,
        grid=(b, HEAD_PAIRS),
        in_specs=[spec] * 3,
        out_specs=pl.BlockSpec((1, t, LANES), lambda i, j: (i, 0, j)),
        out_shape=jax.ShapeDtypeStruct((b, t, ATT_WIDTH), BF16),
        compiler_params=_cparams(("parallel", "parallel"), 32),
        name="ctx_attention",
    )(q, k, v)


def _lat_attn_kernel(q_ref, k_ref, v_ref, kc_ref, vc_ref, t2_ref, o_ref, *, rows_per_step, n_rows):
    rb = pl.program_id(2)
    kc, vc = kc_ref[0, 0], vc_ref[0, 0]
    n_loc = WIN_ROWS * GRID_W
    shape2 = (2 * GRID_W, LANES)
    own_lanes = ((lax.broadcasted_iota(jnp.int32, shape2, 0) < GRID_W)
                 == (lax.broadcasted_iota(jnp.int32, shape2, 1) < HEAD_DIM))
    lo_lanes, _ = _head_masks((GRID_W, LANES))

    def scores(i):
        r = rb * rows_per_step + i
        rs = jnp.clip(r - WIN_ROWS // 2, 0, n_rows - WIN_ROWS)
        off = rs - r + WIN_ROWS - 1
        q = q_ref[0, 0, i * GRID_W:(i + 1) * GRID_W, :]
        q2 = jnp.concatenate([q, q], axis=0)
        q2 = jnp.where(own_lanes, q2, jnp.zeros_like(q2))
        k_start = pl.multiple_of(rs * GRID_W, GRID_W)
        bias = jnp.concatenate([t2_ref[0, off + 2 * j] for j in range(WIN_ROWS // 2)], axis=1)
        s_loc = _scores(q2, k_ref[0, 0, pl.ds(k_start, n_loc), :]) + bias
        s_ctx = _scores(q2, kc)
        return s_loc, s_ctx, k_start

    def finish(i, s_loc, s_ctx, k_start):
        m = jnp.maximum(jnp.max(s_loc, axis=-1, keepdims=True), jnp.max(s_ctx, axis=-1, keepdims=True))
        e_loc = jnp.exp(s_loc - m)
        e_ctx = jnp.exp(s_ctx - m)
        l = jnp.sum(e_loc, axis=-1, keepdims=True) + jnp.sum(e_ctx, axis=-1, keepdims=True)
        o2 = (jnp.dot(e_loc.astype(BF16), v_ref[0, 0, pl.ds(k_start, n_loc), :], preferred_element_type=F32)
              + jnp.dot(e_ctx.astype(BF16), vc, preferred_element_type=F32)) / l
        o_ref[0, i * GRID_W:(i + 1) * GRID_W, :] = jnp.where(lo_lanes, o2[:GRID_W], o2[GRID_W:]).astype(BF16)

    pending = scores(0)
    for i in range(rows_per_step):
        following = scores(i + 1) if i + 1 < rows_per_step else None
        finish(i, *pending)
        pending = following


def _lat_attention(q, k, v, kc, vc, t2):
    b, _, t, _ = q.shape
    lc = kc.shape[2]
    n_rows = t // GRID_W
    rows_per_step = 8
    tq = rows_per_step * GRID_W
    full = pl.BlockSpec((1, 1, t, LANES), lambda i, j, r: (i, j, 0, 0))
    ctx = pl.BlockSpec((1, 1, lc, LANES), lambda i, j, r: (i, j, 0, 0))
    return pl.pallas_call(
        functools.partial(_lat_attn_kernel, rows_per_step=rows_per_step, n_rows=n_rows),
        grid=(b, HEAD_PAIRS, n_rows // rows_per_step),
        in_specs=[
            pl.BlockSpec((1, 1, tq, LANES), lambda i, j, r: (i, j, r, 0)),
            full, full, ctx, ctx,
            pl.BlockSpec((1, N_BIAS_TILES, LANES, LANES), lambda i, j, r: (j, 0, 0, 0)),
        ],
        out_specs=pl.BlockSpec((1, tq, LANES), lambda i, j, r: (i, r, j)),
        out_shape=jax.ShapeDtypeStruct((b, t, ATT_WIDTH), BF16),
        compiler_params=_cparams(("parallel", "parallel", "arbitrary"), 32),
        name="lat_attention",
    )(q, k, v, kc, vc, t2)


def _log_sigmoid(x):
    return jnp.minimum(x, 0.0) - jnp.log1p(jnp.exp(-jnp.abs(x)))


def _lru_kernel(xb_ref, cw_ref, cb_ref, wg_ref, bg_ref, la_ref, h0_ref, *rest, reverse, tc, t_len):
    if reverse:
        out_ref, hfin_ref, a_s, b_s, carry_s = rest
    else:
        hb_ref, yb_ref, out_ref, hfin_ref, a_s, b_s, carry_s = rest
    j = pl.program_id(1)
    n = pl.num_programs(1)
    cj = (n - 1 - j) if reverse else j
    t0 = pl.multiple_of(cj * tc, tc)

    @pl.when(j == 0)
    def _():
        carry_s[...] = jnp.broadcast_to(h0_ref[0], carry_s.shape)

    x = xb_ref[0, pl.ds(t0, tc), :]
    prev = xb_ref[0, pl.ds(pl.multiple_of(jnp.maximum(t0 - SUBLANES, 0), SUBLANES), SUBLANES), :]
    nxt = xb_ref[0, pl.ds(pl.multiple_of(jnp.minimum(t0 + tc, t_len - SUBLANES), SUBLANES), SUBLANES), :]
    prev = jnp.where(t0 > 0, prev, 0.0)
    nxt = jnp.where(t0 + tc < t_len, nxt, 0.0)
    xe = jnp.concatenate([prev, x, nxt], axis=0)
    ne = tc + 2 * SUBLANES

    def shifted(offset):
        return pltpu.roll(xe, (-offset) % ne, 0)[SUBLANES:SUBLANES + tc]

    xc = cb_ref[...] + cw_ref[0:1, :] * shifted(-2)
    xc = xc + cw_ref[1:2, :] * shifted(-1)
    xc = xc + cw_ref[2:3, :] * x
    xc = xc + cw_ref[3:4, :] * shifted(1)

    xcb = xc.astype(BF16)
    half = LRU_WIDTH // 2
    pre = [jnp.dot(xcb[:, hf * half:(hf + 1) * half], wg_ref[hf], preferred_element_type=F32) for hf in range(2)]
    r = jax.nn.sigmoid(jnp.concatenate([pre[0][:, :half], pre[1][:, :half]], axis=1) + bg_ref[0:1, :])
    ig = jax.nn.sigmoid(jnp.concatenate([pre[0][:, half:], pre[1][:, half:]], axis=1) + bg_ref[1:2, :])
    log_a = (LRU_C * r) * _log_sigmoid(la_ref[...])
    a = jnp.exp(log_a)
    a_s[...] = a
    b_s[...] = jnp.sqrt(-jnp.tanh(log_a) * (1.0 + a * a)) * ig * xc

    row = lax.broadcasted_iota(jnp.int32, (SUBLANES, LRU_WIDTH), 0)
    n_tiles = tc // SUBLANES

    def tile_body(g, carry):
        gt = (n_tiles - 1 - g) if reverse else g
        rows = pl.ds(pl.multiple_of(gt * SUBLANES, SUBLANES), SUBLANES)
        a = a_s[rows, :]
        bv = b_s[rows, :]
        for dist in (1, 2, 4):
            if reverse:
                shift, valid = SUBLANES - dist, row < SUBLANES - dist
            else:
                shift, valid = dist, row >= dist
            a_n = pltpu.roll(a, shift, 0)
            b_n = pltpu.roll(bv, shift, 0)
            bv = jnp.where(valid, a * b_n + bv, bv)
            a = jnp.where(valid, a * a_n, a)
        h = bv + a * carry
        b_s[rows, :] = h
        edge = h[0:1, :] if reverse else h[SUBLANES - 1:SUBLANES, :]
        return jnp.broadcast_to(edge, carry.shape)

    carry_s[...] = lax.fori_loop(0, n_tiles, tile_body, carry_s[...])

    if reverse:
        out_ref[0] = b_s[...]
    else:
        out_ref[0] = ((b_s[...] + hb_ref[0]) * jax.nn.gelu(yb_ref[0])).astype(BF16)

    @pl.when(j == n - 1)
    def _():
        hfin_ref[0] = carry_s[0:1, :]


def _lru_pass(xb, conv_w, conv_b, wg, bg, la, h0, hb=None, yb=None):
    b, t, c = xb.shape
    reverse = hb is None
    tc = min(512, t)
    n = t // tc
    chunk_map = (lambda i, j: (i, n - 1 - j, 0)) if reverse else (lambda i, j: (i, j, 0))
    chunk = pl.BlockSpec((1, tc, c), chunk_map)
    in_specs = [
        pl.BlockSpec((1, t, c), lambda i, j: (i, 0, 0)),
        _resident((CONV_W, c)),
        _resident((1, c)),
        _resident((2, c // 2, c)),
        _resident((2, c)),
        _resident((1, c)),
        pl.BlockSpec((1, 1, c), lambda i, j: (i, 0, 0)),
    ]
    args = [xb, conv_w, conv_b, wg, bg, la, h0]
    if not reverse:
        in_specs += [chunk, chunk]
        args += [hb, yb]
    return pl.pallas_call(
        functools.partial(_lru_kernel, reverse=reverse, tc=tc, t_len=t),
        grid=(b, n),
        in_specs=in_specs,
        out_specs=[chunk, pl.BlockSpec((1, 1, c), lambda i, j: (i, 0, 0))],
        out_shape=[jax.ShapeDtypeStruct((b, t, c), F32 if reverse else BF16),
                   jax.ShapeDtypeStruct((b, 1, c), F32)],
        scratch_shapes=[pltpu.VMEM((tc, c), F32), pltpu.VMEM((tc, c), F32), pltpu.VMEM((SUBLANES, c), F32)],
        compiler_params=_cparams(("parallel", "arbitrary"), 48),
        name="lru_bwd" if reverse else "lru_fwd",
    )(*args)


def _out_ffn_kernel(x_ref, att_ref, rec_ref, mod_ref, g_ref, wo_ref, wg_ref, wu_ref, wd_ref, *rest, final_norm):
    o_ref = rest[-1]
    mix = jnp.concatenate([att_ref[0], rec_ref[0]], axis=1)
    x1 = x_ref[0] + mod_ref[0, 2:3, :] * jnp.dot(mix, wo_ref[...], preferred_element_type=F32)
    h2 = (_rmsnorm(x1, g_ref[...]) * (1.0 + mod_ref[0, 4:5, :]) + mod_ref[0, 3:4, :]).astype(BF16)
    gate = jnp.dot(h2, wg_ref[...], preferred_element_type=F32)
    up = jnp.dot(h2, wu_ref[...], preferred_element_type=F32)
    act = ((gate * jax.nn.sigmoid(gate)) * up).astype(BF16)
    x2 = x1 + mod_ref[0, 5:6, :] * jnp.dot(act, wd_ref[...], preferred_element_type=F32)
    if final_norm:
        x2 = _rmsnorm(x2, rest[0][...])
    o_ref[0] = x2


def _out_ffn(x, att, rec, mod, g2, w_out, w_gate, w_up, w_down, g_final=None):
    b, t, d = x.shape
    d_ff = w_gate.shape[1]
    tm = min(256, t)
    shared = mod.shape[0] == 1
    mod_map = (lambda i, j: (0, 0, 0)) if shared else (lambda i, j: (i, 0, 0))
    tok = lambda w: pl.BlockSpec((1, tm, w), lambda i, j: (i, j, 0))
    in_specs = [
        tok(d), tok(ATT_WIDTH), tok(LRU_WIDTH),
        pl.BlockSpec((1, 6, d), mod_map),
        _resident((1, d)),
        _resident((d, d)),
        _resident((d, d_ff)),
        _resident((d, d_ff)),
        _resident((d_ff, d)),
    ]
    args = [x, att, rec, mod, g2, w_out, w_gate, w_up, w_down]
    if g_final is not None:
        in_specs.append(_resident((1, d)))
        args.append(g_final)
    return pl.pallas_call(
        functools.partial(_out_ffn_kernel, final_norm=g_final is not None),
        grid=(b, t // tm),
        in_specs=in_specs,
        out_specs=tok(d),
        out_shape=jax.ShapeDtypeStruct((b, t, d), F32),
        compiler_params=_cparams(("parallel", "parallel"), 56),
        name="out_ffn",
    )(*args)


def _block_diag_halves(w):
    per_half = LRU_BLOCKS // 2
    w4 = w.reshape(2, per_half, LRU_BLOCK, LRU_BLOCK)
    eye = jnp.eye(per_half, dtype=w.dtype)
    return jnp.einsum('hnjk,nm->hnjmk', w4, eye).reshape(2, per_half * LRU_BLOCK, per_half * LRU_BLOCK)


def _head_pair_layout(cache):
    b, depth, p = cache.shape[:3]
    return cache.reshape(b, depth, p, HEAD_PAIRS, LANES).transpose(1, 0, 3, 2, 4).astype(BF16)


def kernel(x_prompt, x_sample, cache_k, cache_v, state_lru, c, c_ctx, w_mod, b_mod, norm1, norm2, w_in, w_out, rpb, conv_w, conv_b, lru_a, lru_wr, lru_br, lru_wi, lru_bi, w_gate, w_up, w_down, norm_final):
    depth = w_in.shape[0]
    n_ctx, n_lat = x_prompt.shape[0], x_sample.shape[0]
    d = x_prompt.shape[-1]
    assert n_lat + 1 <= MOD_ROWS

    cvec = jnp.zeros((MOD_ROWS, d), F32).at[:n_lat].set(c).at[n_lat].set(c_ctx)
    mod = _modulation(cvec, w_mod, b_mod).reshape(depth, MOD_ROWS, 6, d)
    t2 = _bias_tables(rpb)
    ck, cv = _head_pair_layout(cache_k), _head_pair_layout(cache_v)

    w_in_b, w_out_b = w_in.astype(BF16), w_out.astype(BF16)
    w_gate_b, w_up_b, w_down_b = w_gate.astype(BF16), w_up.astype(BF16), w_down.astype(BF16)
    wg = jnp.stack([
        jnp.stack([jnp.concatenate([_block_diag_halves(lru_wr[l, dr]), _block_diag_halves(lru_wi[l, dr])], axis=-1)
                   for dr in range(2)]) for l in range(depth)]).astype(BF16)
    bg = jnp.stack([lru_br, lru_bi], axis=2)
    h0_ctx = jnp.zeros((n_ctx, 1, LRU_WIDTH), F32)

    def layer(x, l, mod_l, attend, h0_f, h0_b, last, emit_kv_f32):
        outs = _in_proj(x, mod_l, norm1[l][None], w_in_b[l], emit_kv_f32=emit_kv_f32)
        q, k, v, xb, yb = outs[:5]
        att = attend(q, k, v)
        lru_args = (xb, conv_w[l], conv_b[l][None])
        hb, hb_fin = _lru_pass(*lru_args, wg[l, 1], bg[l, 1], lru_a[l, 1][None], h0_b)
        rec, hf_fin = _lru_pass(*lru_args, wg[l, 0], bg[l, 0], lru_a[l, 0][None], h0_f, hb=hb, yb=yb)
        x = _out_ffn(x, att, rec, mod_l, norm2[l][None], w_out_b[l], w_gate_b[l], w_up_b[l], w_down_b[l],
                     g_final=norm_final[None] if last else None)
        return x, outs[5:], jnp.concatenate([hf_fin, hb_fin], axis=1)

    xp, xs = x_prompt, x_sample
    ks_out, vs_out, hs_out = [], [], []
    for l in range(depth):
        last = l == depth - 1
        xp, (k_l, v_l), h_l = layer(xp, l, mod[l, n_lat:n_lat + 1], _ctx_attention, h0_ctx, h0_ctx, last, True)
        ks_out.append(k_l.reshape(n_ctx, -1, N_ATT_HEADS, HEAD_DIM))
        vs_out.append(v_l.reshape(n_ctx, -1, N_ATT_HEADS, HEAD_DIM))
        hs_out.append(h_l)
        attend_lat = functools.partial(_lat_attention, kc=ck[l], vc=cv[l], t2=t2[l])
        xs, _, _ = layer(xs, l, mod[l, :n_lat], attend_lat, state_lru[:, l, 0:1], state_lru[:, l, 1:2], last, False)
    return (xp, xs, jnp.stack(ks_out, axis=1), jnp.stack(vs_out, axis=1), jnp.stack(hs_out, axis=1))
```

```python
import functools

import jax
import jax.numpy as jnp
from jax import lax
from jax.experimental import pallas as pl
from jax.experimental.pallas import tpu as pltpu

F32 = jnp.float32
BF16 = jnp.bfloat16

D_MODEL = 1024
GRID_W = 64
N_ATT_HEADS = 8
HEAD_DIM = 64
ATT_WIDTH = N_ATT_HEADS * HEAD_DIM
ATT_SCALE = HEAD_DIM ** -0.5
WIN_ROWS = 8
WIN_COLS = 16
LRU_WIDTH = D_MODEL // 2
LRU_BLOCKS = 8
LRU_BLOCK = LRU_WIDTH // LRU_BLOCKS
CONV_W = 4
CONV_LEFT = 2
LRU_C = 8.0
IN_COLS = 3 * ATT_WIDTH + 2 * LRU_WIDTH
EPS = 1e-6
NEG_INF = -1e30
F32_TINY = float(jnp.finfo(jnp.float32).tiny)

LANES = 128
SUBLANES = 8
HEAD_PAIRS = ATT_WIDTH // LANES
N_ROW_IDX = 2 * WIN_ROWS - 1
N_COL_IDX = 2 * WIN_COLS - 1
N_BIAS_TILES = N_ROW_IDX - 1
MOD_ROWS = 16
LRU_GROUP = SUBLANES
SCAN_PAD = SUBLANES
X_PAD = 3 * SUBLANES
LRU_TIME_BLOCK = 32
MIB = 1024 * 1024


def _cparams(semantics, vmem_mib):
    return pltpu.CompilerParams(dimension_semantics=semantics, vmem_limit_bytes=vmem_mib * MIB)


def _resident(shape):
    zeros = (0,) * len(shape)
    return pl.BlockSpec(shape, lambda *_: zeros, pipeline_mode=pl.Buffered(1))


def _layer_resident(shape, layer):
    index = (layer,) + (0,) * len(shape)
    return pl.BlockSpec((None,) + tuple(shape), lambda *_: index, pipeline_mode=pl.Buffered(1))


def _rmsnorm(x, g):
    return (x * lax.rsqrt(jnp.mean(x * x, axis=-1, keepdims=True) + EPS)) * g


def _mod_kernel(c_ref, w_ref, b_ref, o_ref):
    cv = c_ref[...]
    s = (cv * jax.nn.sigmoid(cv)).astype(BF16)
    o_ref[0] = jnp.dot(s, w_ref[0].astype(BF16), preferred_element_type=F32) + b_ref[0]


def _modulation(cvec, w_mod, b_mod):
    depth, d, n = w_mod.shape
    tn = n // 4
    return pl.pallas_call(
        _mod_kernel,
        grid=(depth, n // tn),
        in_specs=[
            pl.BlockSpec((MOD_ROWS, d), lambda l, j: (0, 0)),
            pl.BlockSpec((1, d, tn), lambda l, j: (l, 0, j)),
            pl.BlockSpec((1, 1, tn), lambda l, j: (l, 0, j)),
        ],
        out_specs=pl.BlockSpec((1, MOD_ROWS, tn), lambda l, j: (l, 0, j)),
        out_shape=jax.ShapeDtypeStruct((depth, MOD_ROWS, n), F32),
        compiler_params=_cparams(("arbitrary", "arbitrary"), 40),
        name="adaln_mod",
    )(cvec, w_mod, b_mod.reshape(depth, 1, n))


def _bias_table_kernel(rpb_ref, o_ref):
    shape = (GRID_W, LANES)
    c = lax.broadcasted_iota(jnp.int32, shape, 0)
    lane = lax.broadcasted_iota(jnp.int32, shape, 1)
    upper = lane >= GRID_W
    kc = jnp.where(upper, lane - GRID_W, lane)
    cs = jnp.clip(c - WIN_COLS // 2, 0, GRID_W - WIN_COLS)
    ok = (kc >= cs) & (kc < cs + WIN_COLS)
    lo_shift = LANES - (WIN_COLS - 1)
    hi_shift = (lo_shift + GRID_W) % LANES
    for h in range(2):
        for i in range(N_BIAS_TILES):
            lo = pltpu.roll(jnp.broadcast_to(rpb_ref[0, h, i:i + 1, :], shape), lo_shift, 1, stride=1, stride_axis=0)
            hi = pltpu.roll(jnp.broadcast_to(rpb_ref[0, h, i + 1:i + 2, :], shape), hi_shift, 1, stride=1, stride_axis=0)
            o_ref[0, i, h * GRID_W:(h + 1) * GRID_W, :] = jnp.where(ok, jnp.where(upper, hi, lo), NEG_INF)


def _bias_tables(rpb):
    depth = rpb.shape[0]
    n = depth * HEAD_PAIRS
    padded = jnp.pad(rpb, ((0, 0), (0, 0), (0, 2 * SUBLANES - N_ROW_IDX), (0, LANES - N_COL_IDX)))
    out = pl.pallas_call(
        _bias_table_kernel,
        grid=(n,),
        in_specs=[pl.BlockSpec((1, 2, 2 * SUBLANES, LANES), lambda i: (i, 0, 0, 0))],
        out_specs=pl.BlockSpec((1, N_BIAS_TILES, LANES, LANES), lambda i: (i, 0, 0, 0)),
        out_shape=jax.ShapeDtypeStruct((n, N_BIAS_TILES, LANES, LANES), F32),
        compiler_params=_cparams(("arbitrary",), 16),
        name="bias_tables",
    )(padded.reshape(n, 2, 2 * SUBLANES, LANES))
    return out.reshape(depth, HEAD_PAIRS, N_BIAS_TILES, LANES, LANES)


def _in_proj_kernel(x_ref, mod_ref, g_ref, w_ref, q_ref, k_ref, v_ref, xb_ref, yb_ref, *kv_f32_refs):
    h = _rmsnorm(x_ref[0], g_ref[...]) * (1.0 + mod_ref[0, 1:2, :]) + mod_ref[0, 0:1, :]
    p = jnp.dot(h.astype(BF16), w_ref[...], preferred_element_type=F32)
    for j in range(HEAD_PAIRS):
        lo = j * LANES
        q_ref[0, j] = (p[:, lo:lo + LANES] * ATT_SCALE).astype(BF16)
        k_ref[0, j] = p[:, ATT_WIDTH + lo:ATT_WIDTH + lo + LANES].astype(BF16)
        v_ref[0, j] = p[:, 2 * ATT_WIDTH + lo:2 * ATT_WIDTH + lo + LANES].astype(BF16)
    xb_ref[0] = p[:, 3 * ATT_WIDTH:3 * ATT_WIDTH + LRU_WIDTH]
    yb_ref[0] = p[:, 3 * ATT_WIDTH + LRU_WIDTH:]
    if kv_f32_refs:
        kf_ref, vf_ref = kv_f32_refs
        kf_ref[0] = p[:, ATT_WIDTH:2 * ATT_WIDTH]
        vf_ref[0] = p[:, 2 * ATT_WIDTH:3 * ATT_WIDTH]


def _in_proj(x, mod, g, w, layer, *, emit_kv_f32):
    b, t, d = x.shape
    tm = min(512, t)
    shared = mod.shape[0] == 1
    mod_map = (lambda i, j: (0, 0, 0)) if shared else (lambda i, j: (i, 0, 0))
    heads_spec = pl.BlockSpec((1, HEAD_PAIRS, tm, LANES), lambda i, j: (i, 0, j, 0))
    heads_shape = jax.ShapeDtypeStruct((b, HEAD_PAIRS, t, LANES), BF16)
    half_spec = pl.BlockSpec((1, tm, LRU_WIDTH), lambda i, j: (i, j, 0))
    half_shape = jax.ShapeDtypeStruct((b, t, LRU_WIDTH), F32)
    n_f32 = 4 if emit_kv_f32 else 2
    return pl.pallas_call(
        _in_proj_kernel,
        grid=(b, t // tm),
        in_specs=[
            pl.BlockSpec((1, tm, d), lambda i, j: (i, j, 0)),
            pl.BlockSpec((1, 6, d), mod_map),
            _resident((1, d)),
            _layer_resident((d, IN_COLS), layer),
        ],
        out_specs=[heads_spec] * 3 + [half_spec] * n_f32,
        out_shape=[heads_shape] * 3 + [half_shape] * n_f32,
        compiler_params=_cparams(("parallel", "parallel"), 48),
        name="in_proj",
    )(x, mod, g, w)


def _head_masks(shape):
    lane = lax.broadcasted_iota(jnp.int32, shape, 1)
    return lane < HEAD_DIM, lane >= HEAD_DIM


def _scores(qh, k):
    return lax.dot_general(qh, k, (((1,), (1,)), ((), ())), preferred_element_type=F32)


def _ctx_attn_kernel(q_ref, k_ref, v_ref, o_ref):
    q, k, v = q_ref[0, 0], k_ref[0, 0], v_ref[0, 0]
    outs = []
    for hm in _head_masks(q.shape):
        s = _scores(jnp.where(hm, q, jnp.zeros_like(q)), k)
        e = jnp.exp(s - jnp.max(s, axis=-1, keepdims=True))
        l = jnp.sum(e, axis=-1, keepdims=True)
        outs.append(jnp.dot(e.astype(BF16), v, preferred_element_type=F32) / l)
    lo, _ = _head_masks(outs[0].shape)
    o_ref[0] = jnp.where(lo, outs[0], outs[1]).astype(BF16)


def _ctx_attention(q, k, v):
    b, _, t, _ = q.shape
    spec = pl.BlockSpec((1, 1, t, LANES), lambda i, j: (i, j, 0, 0))
    return pl.pallas_call(
        _ctx_attn_kernel,
        grid=(b, HEAD_PAIRS),
        in_specs=[spec] * 3,
        out_specs=pl.BlockSpec((1, t, LANES), lambda i, j: (i, 0, j)),
        out_shape=jax.ShapeDtypeStruct((b, t, ATT_WIDTH), BF16),
        compiler_params=_cparams(("parallel", "parallel"), 32),
        name="ctx_attention",
    )(q, k, v)


def _lat_attn_kernel(q_ref, k_ref, v_ref, kc_ref, vc_ref, t2_ref, o_ref, *, rows_per_step, n_rows):
    rb = pl.program_id(2)
    kc, vc = kc_ref[0, 0], vc_ref[0, 0]
    n_loc = WIN_ROWS * GRID_W
    shape2 = (2 * GRID_W, LANES)
    own_lanes = ((lax.broadcasted_iota(jnp.int32, shape2, 0) < GRID_W)
                 == (lax.broadcasted_iota(jnp.int32, shape2, 1) < HEAD_DIM))
    lo_lanes, _ = _head_masks((GRID_W, LANES))

    def scores(i):
        r = rb * rows_per_step + i
        rs = jnp.clip(r - WIN_ROWS // 2, 0, n_rows - WIN_ROWS)
        off = rs - r + WIN_ROWS - 1
        q = q_ref[0, 0, i * GRID_W:(i + 1) * GRID_W, :]
        q2 = jnp.concatenate([q, q], axis=0)
        q2 = jnp.where(own_lanes, q2, jnp.zeros_like(q2))
        k_start = pl.multiple_of(rs * GRID_W, GRID_W)
        bias = jnp.concatenate([t2_ref[0, off + 2 * j] for j in range(WIN_ROWS // 2)], axis=1)
        s_loc = _scores(q2, k_ref[0, 0, pl.ds(k_start, n_loc), :]) + bias
        s_ctx = _scores(q2, kc)
        return jnp.concatenate([s_loc, s_ctx], axis=1), k_start

    def finish(i, s, m, k_start):
        e = jnp.exp(s - m)
        l = jnp.sum(e, axis=-1, keepdims=True)
        e = e.astype(BF16)
        o2 = (jnp.dot(e[:, :n_loc], v_ref[0, 0, pl.ds(k_start, n_loc), :], preferred_element_type=F32)
              + jnp.dot(e[:, n_loc:], vc, preferred_element_type=F32)) / l
        o_ref[0, i * GRID_W:(i + 1) * GRID_W, :] = jnp.where(lo_lanes, o2[:GRID_W], o2[GRID_W:]).astype(BF16)

    rows = [scores(0), scores(1)] + [None] * rows_per_step
    maxima = [jnp.max(rows[0][0], axis=-1, keepdims=True)] + [None] * rows_per_step
    for i in range(rows_per_step):
        if i + 2 < rows_per_step:
            rows[i + 2] = scores(i + 2)
        if i + 1 < rows_per_step:
            maxima[i + 1] = jnp.max(rows[i + 1][0], axis=-1, keepdims=True)
        finish(i, rows[i][0], maxima[i], rows[i][1])
        rows[i] = maxima[i] = None


def _lat_attention(q, k, v, kc, vc, t2):
    b, _, t, _ = q.shape
    lc = kc.shape[2]
    n_rows = t // GRID_W
    rows_per_step = min(32, n_rows)
    tq = rows_per_step * GRID_W
    full = pl.BlockSpec((1, 1, t, LANES), lambda i, j, r: (i, j, 0, 0))
    ctx = pl.BlockSpec((1, 1, lc, LANES), lambda i, j, r: (i, j, 0, 0))
    return pl.pallas_call(
        functools.partial(_lat_attn_kernel, rows_per_step=rows_per_step, n_rows=n_rows),
        grid=(b, HEAD_PAIRS, n_rows // rows_per_step),
        in_specs=[
            pl.BlockSpec((1, 1, tq, LANES), lambda i, j, r: (i, j, r, 0)),
            full, full, ctx, ctx,
            pl.BlockSpec((1, N_BIAS_TILES, LANES, LANES), lambda i, j, r: (j, 0, 0, 0)),
        ],
        out_specs=pl.BlockSpec((1, tq, LANES), lambda i, j, r: (i, r, j)),
        out_shape=jax.ShapeDtypeStruct((b, t, ATT_WIDTH), BF16),
        compiler_params=_cparams(("parallel", "parallel", "arbitrary"), 32),
        name="lat_attention",
    )(q, k, v, kc, vc, t2)


def _log_sigmoid(x):
    return jnp.minimum(x, 0.0) - jnp.log1p(jnp.exp(-jnp.abs(x)))


def _lru_kernel(xb_ref, xprev_ref, xnext_ref, cw_ref, cb_ref, wg_ref, bg_ref, la_ref, h0_ref, *rest, reverse, tc):
    if reverse:
        out_ref, hfin_ref, x_s, a_s, b_s, h_s, carry_s = rest
    else:
        hb_ref, yb_ref, out_ref, hfin_ref, x_s, a_s, b_s, h_s, carry_s = rest
    j = pl.program_id(1)
    n = pl.num_programs(1)
    cj = (n - 1 - j) if reverse else j
    x_pitch = tc + X_PAD
    pitch = tc + SCAN_PAD
    n_slabs = LRU_WIDTH // LANES
    half = LRU_WIDTH // 2
    lane_slab = lambda v, c: v[:, c * LANES:(c + 1) * LANES]

    @pl.when(j == 0)
    def _():
        carry_s[...] = h0_ref[0]

    def park(bi, carry):
        base = pl.multiple_of(bi * x_pitch, SUBLANES)
        pieces = ((0, jnp.where(cj > 0, xprev_ref[bi], 0.0)), (SUBLANES, xb_ref[bi]),
                  (SUBLANES + tc, jnp.where(cj < n - 1, xnext_ref[bi], 0.0)))
        for start, v in pieces:
            for c in range(n_slabs):
                x_s[c, pl.ds(base + start, v.shape[0]), :] = lane_slab(v, c)
        return carry

    lax.fori_loop(0, LRU_GROUP, park, 0)

    neg_log_sig4 = (-0.5 * LRU_C) * _log_sigmoid(la_ref[...])

    def time_rows(t):
        rows = pl.ds(SUBLANES + t, LRU_GROUP, stride=x_pitch)
        return jnp.concatenate([x_s[c, rows, :] for c in range(n_slabs)], axis=1)

    m = LRU_TIME_BLOCK * LRU_GROUP

    def coefficients(blk, carry):
        t0 = blk * LRU_TIME_BLOCK
        xw = jnp.concatenate([time_rows(t0 + k) for k in range(-CONV_LEFT, LRU_TIME_BLOCK + CONV_W - 1 - CONV_LEFT)],
                             axis=0)
        xc = cb_ref[...]
        for tap in range(CONV_W):
            xc = xc + cw_ref[tap:tap + 1, :] * xw[tap * LRU_GROUP:tap * LRU_GROUP + m]
        xcb = xc.astype(BF16)
        pre = [jnp.dot(xcb[:, hf * half:(hf + 1) * half], wg_ref[hf], preferred_element_type=F32)
               for hf in range(2)]
        t_r = jnp.tanh(jnp.concatenate([pre[0][:, :half], pre[1][:, :half]], axis=1) + bg_ref[0:1, :])
        t_i = jnp.tanh(jnp.concatenate([pre[0][:, half:], pre[1][:, half:]], axis=1) + bg_ref[1:2, :])
        neg_log_a = (t_r + 1.0) * neg_log_sig4
        a = jnp.exp(-neg_log_a)
        z = jnp.tanh(neg_log_a) * (1.0 + a * a)
        root = z * lax.rsqrt(jnp.maximum(z, F32_TINY))
        rows = pl.ds(pl.multiple_of(blk * m, m), m)
        a_s[rows, :] = a
        b_s[rows, :] = (root * (0.5 * t_i + 0.5)) * xc
        return carry

    lax.fori_loop(0, tc // LRU_TIME_BLOCK, coefficients, 0)

    def step(i, h):
        t = (tc - 1 - i) if reverse else i
        rows = pl.ds(pl.multiple_of(t * LRU_GROUP, LRU_GROUP), LRU_GROUP)
        h = a_s[rows, :] * h + b_s[rows, :]
        for c in range(n_slabs):
            h_s[c, pl.ds(t, LRU_GROUP, stride=pitch), :] = lane_slab(h, c)
        return h

    carry_s[...] = lax.fori_loop(0, tc, step, carry_s[...], unroll=8)

    def emit(bi, carry):
        rows = pl.ds(pl.multiple_of(bi * pitch, SUBLANES), tc)
        hs = jnp.concatenate([h_s[c, rows, :] for c in range(n_slabs)], axis=1)
        if reverse:
            out_ref[bi] = hs
        else:
            out_ref[bi] = ((hs + hb_ref[bi]) * jax.nn.gelu(yb_ref[bi])).astype(BF16)
        return carry

    lax.fori_loop(0, LRU_GROUP, emit, 0)

    @pl.when(j == n - 1)
    def _():
        hfin_ref[0] = carry_s[...]


def _lru_pass(xb, conv_w, conv_b, wg, bg, la, h0, hb=None, yb=None):
    b, t, c = xb.shape
    assert b % LRU_GROUP == 0
    groups = b // LRU_GROUP
    reverse = hb is None
    tc = min(256, t)
    n = t // tc
    per = tc // SUBLANES
    pos = (lambda j: n - 1 - j) if reverse else (lambda j: j)
    chunk = pl.BlockSpec((LRU_GROUP, tc, c), lambda g, j: (g, pos(j), 0))
    halo = lambda f: pl.BlockSpec((LRU_GROUP, SUBLANES, c), lambda g, j: (g, f(pos(j)), 0))
    state = pl.BlockSpec((1, LRU_GROUP, c), lambda g, j: (g, 0, 0))
    in_specs = [
        chunk,
        halo(lambda cj: jnp.maximum(cj * per - 1, 0)),
        halo(lambda cj: jnp.minimum((cj + 1) * per, t // SUBLANES - 1)),
        _resident((CONV_W, c)),
        _resident((1, c)),
        _resident((2, c // 2, c)),
        _resident((2, c)),
        _resident((1, c)),
        state,
    ]
    args = [xb, xb, xb, conv_w, conv_b, wg, bg, la, h0.reshape(groups, LRU_GROUP, c)]
    if not reverse:
        in_specs += [chunk, chunk]
        args += [hb, yb]
    assert tc % (2 * SUBLANES) == 0 and tc % LRU_TIME_BLOCK == 0
    slabs = lambda pad: pltpu.VMEM((c // LANES, LRU_GROUP * (tc + pad), LANES), F32)
    time_major = pltpu.VMEM((tc * LRU_GROUP, c), F32)
    out, hfin = pl.pallas_call(
        functools.partial(_lru_kernel, reverse=reverse, tc=tc),
        grid=(groups, n),
        in_specs=in_specs,
        out_specs=[chunk, state],
        out_shape=[jax.ShapeDtypeStruct((b, t, c), F32 if reverse else BF16),
                   jax.ShapeDtypeStruct((groups, LRU_GROUP, c), F32)],
        scratch_shapes=[slabs(X_PAD), time_major, time_major, slabs(SCAN_PAD), pltpu.VMEM((LRU_GROUP, c), F32)],
        compiler_params=_cparams(("parallel", "arbitrary"), 56),
        name="lru_bwd" if reverse else "lru_fwd",
    )(*args)
    return out, hfin.reshape(b, c)


def _out_ffn_kernel(x_ref, att_ref, rec_ref, mod_ref, g_ref, wo_ref, wg_ref, wu_ref, wd_ref, *rest, final_norm):
    o_ref = rest[-1]
    mix = jnp.concatenate([att_ref[0], rec_ref[0]], axis=1)
    x1 = x_ref[0] + mod_ref[0, 2:3, :] * jnp.dot(mix, wo_ref[...], preferred_element_type=F32)
    h2 = (_rmsnorm(x1, g_ref[...]) * (1.0 + mod_ref[0, 4:5, :]) + mod_ref[0, 3:4, :]).astype(BF16)
    gate = jnp.dot(h2, wg_ref[...], preferred_element_type=F32)
    up = jnp.dot(h2, wu_ref[...], preferred_element_type=F32)
    act = ((gate * jax.nn.sigmoid(gate)) * up).astype(BF16)
    x2 = x1 + mod_ref[0, 5:6, :] * jnp.dot(act, wd_ref[...], preferred_element_type=F32)
    if final_norm:
        x2 = _rmsnorm(x2, rest[0][...])
    o_ref[0] = x2


def _out_ffn(x, att, rec, mod, g2, w_out, w_gate, w_up, w_down, layer, g_final=None):
    b, t, d = x.shape
    d_ff = w_gate.shape[-1]
    tm = min(256, t)
    shared = mod.shape[0] == 1
    mod_map = (lambda i, j: (0, 0, 0)) if shared else (lambda i, j: (i, 0, 0))
    tok = lambda w: pl.BlockSpec((1, tm, w), lambda i, j: (i, j, 0))
    in_specs = [
        tok(d), tok(ATT_WIDTH), tok(LRU_WIDTH),
        pl.BlockSpec((1, 6, d), mod_map),
        _resident((1, d)),
        _layer_resident((d, d), layer),
        _layer_resident((d, d_ff), layer),
        _layer_resident((d, d_ff), layer),
        _layer_resident((d_ff, d), layer),
    ]
    args = [x, att, rec, mod, g2, w_out, w_gate, w_up, w_down]
    if g_final is not None:
        in_specs.append(_resident((1, d)))
        args.append(g_final)
    return pl.pallas_call(
        functools.partial(_out_ffn_kernel, final_norm=g_final is not None),
        grid=(b, t // tm),
        in_specs=in_specs,
        out_specs=tok(d),
        out_shape=jax.ShapeDtypeStruct((b, t, d), F32),
        compiler_params=_cparams(("parallel", "parallel"), 56),
        name="out_ffn",
    )(*args)


def _block_diag_halves(w):
    per_half = LRU_BLOCKS // 2
    w4 = w.reshape(2, per_half, LRU_BLOCK, LRU_BLOCK)
    eye = jnp.eye(per_half, dtype=w.dtype)
    return jnp.einsum('hnjk,nm->hnjmk', w4, eye).reshape(2, per_half * LRU_BLOCK, per_half * LRU_BLOCK)


def _head_pair_layout(cache):
    b, depth, p = cache.shape[:3]
    return cache.reshape(b, depth, p, HEAD_PAIRS, LANES).transpose(1, 0, 3, 2, 4).astype(BF16)


def kernel(x_prompt, x_sample, cache_k, cache_v, state_lru, c, c_ctx, w_mod, b_mod, norm1, norm2, w_in, w_out, rpb, conv_w, conv_b, lru_a, lru_wr, lru_br, lru_wi, lru_bi, w_gate, w_up, w_down, norm_final):
    depth = w_in.shape[0]
    n_ctx, n_lat = x_prompt.shape[0], x_sample.shape[0]
    d = x_prompt.shape[-1]
    assert n_lat + 1 <= MOD_ROWS

    cvec = jnp.zeros((MOD_ROWS, d), F32).at[:n_lat].set(c).at[n_lat].set(c_ctx)
    mod = _modulation(cvec, w_mod, b_mod).reshape(depth, MOD_ROWS, 6, d)
    t2 = _bias_tables(rpb)
    ck, cv = _head_pair_layout(cache_k), _head_pair_layout(cache_v)

    w_in_b, w_out_b = w_in.astype(BF16), w_out.astype(BF16)
    w_gate_b, w_up_b, w_down_b = w_gate.astype(BF16), w_up.astype(BF16), w_down.astype(BF16)
    wg = (0.5 * jnp.stack([
        jnp.stack([jnp.concatenate([_block_diag_halves(lru_wr[l, dr]), _block_diag_halves(lru_wi[l, dr])], axis=-1)
                   for dr in range(2)]) for l in range(depth)])).astype(BF16)
    bg = 0.5 * jnp.stack([lru_br, lru_bi], axis=2)
    h0_ctx = jnp.zeros((n_ctx, LRU_WIDTH), F32)

    def layer(x, l, mod_l, attend, h0_f, h0_b, last, emit_kv_f32):
        outs = _in_proj(x, mod_l, norm1[l][None], w_in_b, l, emit_kv_f32=emit_kv_f32)
        q, k, v, xb, yb = outs[:5]
        att = attend(q, k, v)
        lru_args = (xb, conv_w[l], conv_b[l][None])
        hb, hb_fin = _lru_pass(*lru_args, wg[l, 1], bg[l, 1], lru_a[l, 1][None], h0_b)
        rec, hf_fin = _lru_pass(*lru_args, wg[l, 0], bg[l, 0], lru_a[l, 0][None], h0_f, hb=hb, yb=yb)
        x = _out_ffn(x, att, rec, mod_l, norm2[l][None], w_out_b, w_gate_b, w_up_b, w_down_b, l,
                     g_final=norm_final[None] if last else None)
        return x, outs[5:], jnp.stack([hf_fin, hb_fin], axis=1)

    xp, xs = x_prompt, x_sample
    ks_out, vs_out, hs_out = [], [], []
    for l in range(depth):
        last = l == depth - 1
        xp, (k_l, v_l), h_l = layer(xp, l, mod[l, n_lat:n_lat + 1], _ctx_attention, h0_ctx, h0_ctx, last, True)
        ks_out.append(k_l.reshape(n_ctx, -1, N_ATT_HEADS, HEAD_DIM))
        vs_out.append(v_l.reshape(n_ctx, -1, N_ATT_HEADS, HEAD_DIM))
        hs_out.append(h_l)
        attend_lat = functools.partial(_lat_attention, kc=ck[l], vc=cv[l], t2=t2[l])
        xs, _, _ = layer(xs, l, mod[l, :n_lat], attend_lat, state_lru[:, l, 0], state_lru[:, l, 1], last, False)
    return (xp, xs, jnp.stack(ks_out, axis=1), jnp.stack(vs_out, axis=1), jnp.stack(hs_out, axis=1))
```

```python
import functools

import jax
import jax.numpy as jnp
from jax import lax
from jax.experimental import pallas as pl
from jax.experimental.pallas import tpu as pltpu

F32 = jnp.float32
BF16 = jnp.bfloat16

D_MODEL = 1024
GRID_W = 64
N_ATT_HEADS = 8
HEAD_DIM = 64
ATT_WIDTH = N_ATT_HEADS * HEAD_DIM
ATT_SCALE = HEAD_DIM ** -0.5
WIN_ROWS = 8
WIN_COLS = 16
LRU_WIDTH = D_MODEL // 2
LRU_BLOCKS = 8
LRU_BLOCK = LRU_WIDTH // LRU_BLOCKS
CONV_W = 4
CONV_LEFT = 2
LRU_C = 8.0
IN_COLS = 3 * ATT_WIDTH + 2 * LRU_WIDTH
EPS = 1e-6
NEG_INF = -1e30
F32_TINY = float(jnp.finfo(jnp.float32).tiny)
LOG2_E = 1.4426950408889634

LANES = 128
SUBLANES = 8
HEAD_PAIRS = ATT_WIDTH // LANES
N_ROW_IDX = 2 * WIN_ROWS - 1
N_COL_IDX = 2 * WIN_COLS - 1
N_BIAS_TILES = N_ROW_IDX - 1
MOD_ROWS = 16
LRU_GROUP = SUBLANES
SCAN_PAD = SUBLANES
X_PAD = 3 * SUBLANES
LRU_TIME_BLOCK = 32
MIB = 1024 * 1024


def _cparams(semantics, vmem_mib):
    return pltpu.CompilerParams(dimension_semantics=semantics, vmem_limit_bytes=vmem_mib * MIB)


def _resident(shape):
    zeros = (0,) * len(shape)
    return pl.BlockSpec(shape, lambda *_: zeros, pipeline_mode=pl.Buffered(1))


def _layer_resident(shape, layer):
    index = (layer,) + (0,) * len(shape)
    return pl.BlockSpec((None,) + tuple(shape), lambda *_: index, pipeline_mode=pl.Buffered(1))


def _rmsnorm(x, g):
    return (x * lax.rsqrt(jnp.mean(x * x, axis=-1, keepdims=True) + EPS)) * g


def _mod_kernel(c_ref, w_ref, b_ref, o_ref):
    cv = c_ref[...]
    s = (cv * jax.nn.sigmoid(cv)).astype(BF16)
    o_ref[0] = jnp.dot(s, w_ref[0].astype(BF16), preferred_element_type=F32) + b_ref[0]


def _modulation(cvec, w_mod, b_mod):
    depth, d, n = w_mod.shape
    tn = n // 4
    return pl.pallas_call(
        _mod_kernel,
        grid=(depth, n // tn),
        in_specs=[
            pl.BlockSpec((MOD_ROWS, d), lambda l, j: (0, 0)),
            pl.BlockSpec((1, d, tn), lambda l, j: (l, 0, j)),
            pl.BlockSpec((1, 1, tn), lambda l, j: (l, 0, j)),
        ],
        out_specs=pl.BlockSpec((1, MOD_ROWS, tn), lambda l, j: (l, 0, j)),
        out_shape=jax.ShapeDtypeStruct((depth, MOD_ROWS, n), F32),
        compiler_params=_cparams(("arbitrary", "arbitrary"), 40),
        name="adaln_mod",
    )(cvec, w_mod, b_mod.reshape(depth, 1, n))


def _bias_table_kernel(rpb_ref, o_ref):
    shape = (GRID_W, LANES)
    c = lax.broadcasted_iota(jnp.int32, shape, 0)
    lane = lax.broadcasted_iota(jnp.int32, shape, 1)
    upper = lane >= GRID_W
    kc = jnp.where(upper, lane - GRID_W, lane)
    cs = jnp.clip(c - WIN_COLS // 2, 0, GRID_W - WIN_COLS)
    ok = (kc >= cs) & (kc < cs + WIN_COLS)
    lo_shift = LANES - (WIN_COLS - 1)
    hi_shift = (lo_shift + GRID_W) % LANES
    for h in range(2):
        for i in range(N_BIAS_TILES):
            lo = pltpu.roll(jnp.broadcast_to(rpb_ref[0, h, i:i + 1, :], shape), lo_shift, 1, stride=1, stride_axis=0)
            hi = pltpu.roll(jnp.broadcast_to(rpb_ref[0, h, i + 1:i + 2, :], shape), hi_shift, 1, stride=1, stride_axis=0)
            o_ref[0, i, h * GRID_W:(h + 1) * GRID_W, :] = jnp.where(ok, jnp.where(upper, hi, lo), NEG_INF)


def _bias_tables(rpb):
    depth = rpb.shape[0]
    n = depth * HEAD_PAIRS
    padded = jnp.pad(rpb, ((0, 0), (0, 0), (0, 2 * SUBLANES - N_ROW_IDX), (0, LANES - N_COL_IDX)))
    out = pl.pallas_call(
        _bias_table_kernel,
        grid=(n,),
        in_specs=[pl.BlockSpec((1, 2, 2 * SUBLANES, LANES), lambda i: (i, 0, 0, 0))],
        out_specs=pl.BlockSpec((1, N_BIAS_TILES, LANES, LANES), lambda i: (i, 0, 0, 0)),
        out_shape=jax.ShapeDtypeStruct((n, N_BIAS_TILES, LANES, LANES), F32),
        compiler_params=_cparams(("arbitrary",), 16),
        name="bias_tables",
    )(padded.reshape(n, 2, 2 * SUBLANES, LANES))
    return out.reshape(depth, HEAD_PAIRS, N_BIAS_TILES, LANES, LANES)


def _in_proj_kernel(x_ref, mod_ref, g_ref, w_ref, q_ref, k_ref, v_ref, xb_ref, gy_ref, *kv_f32_refs):
    h = _rmsnorm(x_ref[0], g_ref[...]) * (1.0 + mod_ref[0, 1:2, :]) + mod_ref[0, 0:1, :]
    p = jnp.dot(h.astype(BF16), w_ref[...], preferred_element_type=F32)
    for j in range(HEAD_PAIRS):
        lo = j * LANES
        q_ref[0, j] = (p[:, lo:lo + LANES] * ATT_SCALE).astype(BF16)
        k_ref[0, j] = p[:, ATT_WIDTH + lo:ATT_WIDTH + lo + LANES].astype(BF16)
        v_ref[0, j] = p[:, 2 * ATT_WIDTH + lo:2 * ATT_WIDTH + lo + LANES].astype(BF16)
    xb_ref[0] = p[:, 3 * ATT_WIDTH:3 * ATT_WIDTH + LRU_WIDTH]
    gy_ref[0] = jax.nn.gelu(p[:, 3 * ATT_WIDTH + LRU_WIDTH:])
    if kv_f32_refs:
        kf_ref, vf_ref = kv_f32_refs
        kf_ref[0] = p[:, ATT_WIDTH:2 * ATT_WIDTH]
        vf_ref[0] = p[:, 2 * ATT_WIDTH:3 * ATT_WIDTH]


def _in_proj(x, mod, g, w, layer, *, emit_kv_f32):
    b, t, d = x.shape
    tm = min(512, t)
    shared = mod.shape[0] == 1
    mod_map = (lambda i, j: (0, 0, 0)) if shared else (lambda i, j: (i, 0, 0))
    heads_spec = pl.BlockSpec((1, HEAD_PAIRS, tm, LANES), lambda i, j: (i, 0, j, 0))
    heads_shape = jax.ShapeDtypeStruct((b, HEAD_PAIRS, t, LANES), BF16)
    half_spec = pl.BlockSpec((1, tm, LRU_WIDTH), lambda i, j: (i, j, 0))
    half_shape = jax.ShapeDtypeStruct((b, t, LRU_WIDTH), F32)
    n_f32 = 4 if emit_kv_f32 else 2
    return pl.pallas_call(
        _in_proj_kernel,
        grid=(b, t // tm),
        in_specs=[
            pl.BlockSpec((1, tm, d), lambda i, j: (i, j, 0)),
            pl.BlockSpec((1, 6, d), mod_map),
            _resident((1, d)),
            _layer_resident((d, IN_COLS), layer),
        ],
        out_specs=[heads_spec] * 3 + [half_spec] * n_f32,
        out_shape=[heads_shape] * 3 + [half_shape] * n_f32,
        compiler_params=_cparams(("parallel", "parallel"), 48),
        name="in_proj",
    )(x, mod, g, w)


def _head_masks(shape):
    lane = lax.broadcasted_iota(jnp.int32, shape, 1)
    return lane < HEAD_DIM, lane >= HEAD_DIM


def _scores(qh, k):
    return lax.dot_general(qh, k, (((1,), (1,)), ((), ())), preferred_element_type=F32)


def _ctx_attn_kernel(q_ref, k_ref, v_ref, o_ref):
    q, k, v = q_ref[0, 0], k_ref[0, 0], v_ref[0, 0]
    outs = []
    for hm in _head_masks(q.shape):
        s = _scores(jnp.where(hm, q, jnp.zeros_like(q)), k)
        e = jnp.exp(s - jnp.max(s, axis=-1, keepdims=True))
        l = jnp.sum(e, axis=-1, keepdims=True)
        outs.append(jnp.dot(e.astype(BF16), v, preferred_element_type=F32) / l)
    lo, _ = _head_masks(outs[0].shape)
    o_ref[0] = jnp.where(lo, outs[0], outs[1]).astype(BF16)


def _ctx_attention(q, k, v):
    b, _, t, _ = q.shape
    spec = pl.BlockSpec((1, 1, t, LANES), lambda i, j: (i, j, 0, 0))
    return pl.pallas_call(
        _ctx_attn_kernel,
        grid=(b, HEAD_PAIRS),
        in_specs=[spec] * 3,
        out_specs=pl.BlockSpec((1, t, LANES), lambda i, j: (i, 0, j)),
        out_shape=jax.ShapeDtypeStruct((b, t, ATT_WIDTH), BF16),
        compiler_params=_cparams(("parallel", "parallel"), 32),
        name="ctx_attention",
    )(q, k, v)


def _lat_attn_kernel(q_ref, k_ref, v_ref, kc_ref, vc_ref, t2_ref, o_ref, *, rows_per_step, n_rows):
    rb = pl.program_id(2)
    kc, vc = kc_ref[0, 0], vc_ref[0, 0]
    n_loc = WIN_ROWS * GRID_W
    shape2 = (2 * GRID_W, LANES)
    own_lanes = ((lax.broadcasted_iota(jnp.int32, shape2, 0) < GRID_W)
                 == (lax.broadcasted_iota(jnp.int32, shape2, 1) < HEAD_DIM))
    lo_lanes, _ = _head_masks((GRID_W, LANES))

    def scores(i):
        r = rb * rows_per_step + i
        rs = jnp.clip(r - WIN_ROWS // 2, 0, n_rows - WIN_ROWS)
        off = rs - r + WIN_ROWS - 1
        q = q_ref[0, 0, i * GRID_W:(i + 1) * GRID_W, :]
        q2 = jnp.concatenate([q, q], axis=0)
        q2 = jnp.where(own_lanes, q2, jnp.zeros_like(q2))
        k_start = pl.multiple_of(rs * GRID_W, GRID_W)
        bias = jnp.concatenate([t2_ref[0, off + 2 * j] for j in range(WIN_ROWS // 2)], axis=1)
        s_loc = _scores(q2, k_ref[0, 0, pl.ds(k_start, n_loc), :]) + bias
        s_ctx = _scores(q2, kc)
        return jnp.concatenate([s_loc, s_ctx], axis=1), k_start

    def finish(i, s, m, k_start):
        e = jnp.exp(s - m)
        l = jnp.sum(e, axis=-1, keepdims=True)
        e = e.astype(BF16)
        o2 = (jnp.dot(e[:, :n_loc], v_ref[0, 0, pl.ds(k_start, n_loc), :], preferred_element_type=F32)
              + jnp.dot(e[:, n_loc:], vc, preferred_element_type=F32)) / l
        o_ref[0, i * GRID_W:(i + 1) * GRID_W, :] = jnp.where(lo_lanes, o2[:GRID_W], o2[GRID_W:]).astype(BF16)

    rows = [scores(0), scores(1)] + [None] * rows_per_step
    maxima = [jnp.max(rows[0][0], axis=-1, keepdims=True)] + [None] * rows_per_step
    for i in range(rows_per_step):
        if i + 2 < rows_per_step:
            rows[i + 2] = scores(i + 2)
        if i + 1 < rows_per_step:
            maxima[i + 1] = jnp.max(rows[i + 1][0], axis=-1, keepdims=True)
        finish(i, rows[i][0], maxima[i], rows[i][1])
        rows[i] = maxima[i] = None


def _lat_attention(q, k, v, kc, vc, t2):
    b, _, t, _ = q.shape
    lc = kc.shape[2]
    n_rows = t // GRID_W
    rows_per_step = min(32, n_rows)
    tq = rows_per_step * GRID_W
    full = pl.BlockSpec((1, 1, t, LANES), lambda i, j, r: (i, j, 0, 0))
    ctx = pl.BlockSpec((1, 1, lc, LANES), lambda i, j, r: (i, j, 0, 0))
    return pl.pallas_call(
        functools.partial(_lat_attn_kernel, rows_per_step=rows_per_step, n_rows=n_rows),
        grid=(b, HEAD_PAIRS, n_rows // rows_per_step),
        in_specs=[
            pl.BlockSpec((1, 1, tq, LANES), lambda i, j, r: (i, j, r, 0)),
            full, full, ctx, ctx,
            pl.BlockSpec((1, N_BIAS_TILES, LANES, LANES), lambda i, j, r: (j, 0, 0, 0)),
        ],
        out_specs=pl.BlockSpec((1, tq, LANES), lambda i, j, r: (i, r, j)),
        out_shape=jax.ShapeDtypeStruct((b, t, ATT_WIDTH), BF16),
        compiler_params=_cparams(("parallel", "parallel", "arbitrary"), 32),
        name="lat_attention",
    )(q, k, v, kc, vc, t2)


def _log_sigmoid(x):
    return jnp.minimum(x, 0.0) - jnp.log1p(jnp.exp(-jnp.abs(x)))


def _gate_matmuls(xh, wg_ref):
    half = LRU_WIDTH // 2
    xhb = xh.astype(BF16)
    return [jnp.dot(xhb[:, hf * half:(hf + 1) * half], wg_ref[hf], preferred_element_type=F32) for hf in range(2)]


def _lru_coefficients(pre, xh, bg_ref, neg_log_sig4, neg_log2_sig4):
    half = LRU_WIDTH // 2
    t_r = jnp.tanh(jnp.concatenate([pre[0][:, :half], pre[1][:, :half]], axis=1) + bg_ref[0:1, :])
    t_i = jnp.tanh(jnp.concatenate([pre[0][:, half:], pre[1][:, half:]], axis=1) + bg_ref[1:2, :])
    gate = t_r + 1.0
    a = jnp.exp2(gate * neg_log2_sig4)
    z = jnp.tanh(gate * neg_log_sig4) * (1.0 + a * a)
    root = z * lax.rsqrt(jnp.maximum(z, F32_TINY))
    return a, (root * (t_i + 1.0)) * xh


def _pipelined_coefficients(n_blocks, gate_inputs, xh_block, bg_ref, scales, a_s, b_s):
    m = LRU_TIME_BLOCK * LRU_GROUP
    pre = gate_inputs(0)
    for blk in range(n_blocks):
        following = gate_inputs(blk + 1) if blk + 1 < n_blocks else None
        rows = slice(blk * m, (blk + 1) * m)
        a_s[rows, :], b_s[rows, :] = _lru_coefficients(pre, xh_block(rows), bg_ref, *scales)
        pre = following


def _log_sigmoid_scales(la_ref):
    neg_log_sig4 = (-0.5 * LRU_C) * _log_sigmoid(la_ref[...])
    return neg_log_sig4, neg_log_sig4 * (-LOG2_E)


def _lru_bwd_kernel(xb_ref, xprev_ref, xnext_ref, cw_ref, cb_ref, wg_ref, bg_ref, la_ref, h0_ref,
                    xh_ref, hb_ref, hfin_ref, x_s, a_s, b_s, carry_s, *, tc):
    j = pl.program_id(1)
    n = pl.num_programs(1)
    cj = n - 1 - j
    x_pitch = tc + X_PAD
    n_slabs = LRU_WIDTH // LANES
    lane_slab = lambda v, c: v[:, c * LANES:(c + 1) * LANES]

    @pl.when(j == 0)
    def _():
        carry_s[...] = h0_ref[0]

    def park(bi, carry):
        base = pl.multiple_of(bi * x_pitch, SUBLANES)
        pieces = ((0, jnp.where(cj > 0, xprev_ref[bi], 0.0)), (SUBLANES, xb_ref[bi]),
                  (SUBLANES + tc, jnp.where(cj < n - 1, xnext_ref[bi], 0.0)))
        for start, v in pieces:
            for c in range(n_slabs):
                x_s[c, pl.ds(base + start, v.shape[0]), :] = lane_slab(v, c)
        return carry

    lax.fori_loop(0, LRU_GROUP, park, 0)

    scales = _log_sigmoid_scales(la_ref)

    def time_rows(t):
        rows = pl.ds(SUBLANES + t, LRU_GROUP, stride=x_pitch)
        return jnp.concatenate([x_s[c, rows, :] for c in range(n_slabs)], axis=1)

    m = LRU_TIME_BLOCK * LRU_GROUP

    def conv_block(blk):
        t0 = blk * LRU_TIME_BLOCK
        xw = jnp.concatenate([time_rows(t0 + k) for k in range(-CONV_LEFT, LRU_TIME_BLOCK + CONV_W - 1 - CONV_LEFT)],
                             axis=0)
        xh = cb_ref[...]
        for tap in range(CONV_W):
            xh = xh + cw_ref[tap:tap + 1, :] * xw[tap * LRU_GROUP:tap * LRU_GROUP + m]
        xh_ref[0, blk * m:(blk + 1) * m, :] = xh
        return _gate_matmuls(xh, wg_ref)

    _pipelined_coefficients(tc // LRU_TIME_BLOCK, conv_block, lambda rows: xh_ref[0, rows, :], bg_ref, scales, a_s, b_s)

    def step(i, h):
        rows = pl.ds(pl.multiple_of((tc - 1 - i) * LRU_GROUP, LRU_GROUP), LRU_GROUP)
        h = a_s[rows, :] * h + b_s[rows, :]
        hb_ref[0, rows, :] = h
        return h

    carry_s[...] = lax.fori_loop(0, tc, step, carry_s[...], unroll=8)

    @pl.when(j == n - 1)
    def _():
        hfin_ref[0] = carry_s[...]


def _lru_fwd_kernel(xh_ref, hb_ref, gy_ref, wg_ref, bg_ref, la_ref, h0_ref, out_ref, hfin_ref,
                    a_s, b_s, s_s, carry_s, *, tc):
    j = pl.program_id(1)
    n = pl.num_programs(1)
    pitch = tc + SCAN_PAD
    n_slabs = LRU_WIDTH // LANES
    m = LRU_TIME_BLOCK * LRU_GROUP

    @pl.when(j == 0)
    def _():
        carry_s[...] = h0_ref[0]

    scales = _log_sigmoid_scales(la_ref)

    xh_block = lambda rows: xh_ref[0, rows, :]
    _pipelined_coefficients(tc // LRU_TIME_BLOCK, lambda blk: _gate_matmuls(xh_block(slice(blk * m, (blk + 1) * m)), wg_ref),
                            xh_block, bg_ref, scales, a_s, b_s)

    def step(t, h):
        rows = pl.ds(pl.multiple_of(t * LRU_GROUP, LRU_GROUP), LRU_GROUP)
        h = a_s[rows, :] * h + b_s[rows, :]
        both = h + hb_ref[0, rows, :]
        for c in range(n_slabs):
            s_s[c, pl.ds(t, LRU_GROUP, stride=pitch), :] = both[:, c * LANES:(c + 1) * LANES]
        return h

    carry_s[...] = lax.fori_loop(0, tc, step, carry_s[...], unroll=8)

    def emit(bi, carry):
        rows = pl.ds(pl.multiple_of(bi * pitch, SUBLANES), tc)
        both = jnp.concatenate([s_s[c, rows, :] for c in range(n_slabs)], axis=1)
        out_ref[bi] = (both * gy_ref[bi]).astype(BF16)
        return carry

    lax.fori_loop(0, LRU_GROUP, emit, 0)

    @pl.when(j == n - 1)
    def _():
        hfin_ref[0] = carry_s[...]


def _lru(xb, gy, conv_w, conv_b, wg, bg, la, h0):
    b, t, c = xb.shape
    assert b % LRU_GROUP == 0
    groups = b // LRU_GROUP
    tc = min(256, t)
    n = t // tc
    per = tc // SUBLANES
    assert tc % (2 * SUBLANES) == 0 and tc % LRU_TIME_BLOCK == 0
    h0 = h0.reshape(groups, LRU_GROUP, 2, c)
    state = pl.BlockSpec((1, LRU_GROUP, c), lambda g, j: (g, 0, 0))
    state_shape = jax.ShapeDtypeStruct((groups, LRU_GROUP, c), F32)
    tm_rows = tc * LRU_GROUP
    tm_shape = jax.ShapeDtypeStruct((groups, t * LRU_GROUP, c), F32)
    slabs = lambda pad: pltpu.VMEM((c // LANES, LRU_GROUP * (tc + pad), LANES), F32)
    tm_scratch = pltpu.VMEM((tm_rows, c), F32)
    carry = pltpu.VMEM((LRU_GROUP, c), F32)
    gate_specs = [_resident((2, c // 2, c)), _resident((2, c)), _resident((1, c)), state]

    rev = lambda j: n - 1 - j
    rev_tm = pl.BlockSpec((1, tm_rows, c), lambda g, j: (g, rev(j), 0))
    halo = lambda f: pl.BlockSpec((LRU_GROUP, SUBLANES, c), lambda g, j: (g, f(rev(j)), 0))
    xh, hb, hb_fin = pl.pallas_call(
        functools.partial(_lru_bwd_kernel, tc=tc),
        grid=(groups, n),
        in_specs=[
            pl.BlockSpec((LRU_GROUP, tc, c), lambda g, j: (g, rev(j), 0)),
            halo(lambda cj: jnp.maximum(cj * per - 1, 0)),
            halo(lambda cj: jnp.minimum((cj + 1) * per, t // SUBLANES - 1)),
            _resident((CONV_W, c)),
            _resident((1, c)),
        ] + gate_specs,
        out_specs=[rev_tm, rev_tm, state],
        out_shape=[tm_shape, tm_shape, state_shape],
        scratch_shapes=[slabs(X_PAD), tm_scratch, tm_scratch, carry],
        compiler_params=_cparams(("parallel", "arbitrary"), 56),
        name="lru_bwd",
    )(xb, xb, xb, conv_w, conv_b, wg[1], bg[1], la[1:2], h0[:, :, 1])

    fwd_tm = pl.BlockSpec((1, tm_rows, c), lambda g, j: (g, j, 0))
    chunk = pl.BlockSpec((LRU_GROUP, tc, c), lambda g, j: (g, j, 0))
    rec, hf_fin = pl.pallas_call(
        functools.partial(_lru_fwd_kernel, tc=tc),
        grid=(groups, n),
        in_specs=[fwd_tm, fwd_tm, chunk] + gate_specs,
        out_specs=[chunk, state],
        out_shape=[jax.ShapeDtypeStruct((b, t, c), BF16), state_shape],
        scratch_shapes=[tm_scratch, tm_scratch, slabs(SCAN_PAD), carry],
        compiler_params=_cparams(("parallel", "arbitrary"), 56),
        name="lru_fwd",
    )(xh, hb, gy, wg[0], bg[0], la[0:1], h0[:, :, 0])
    return rec, jnp.stack([hf_fin.reshape(b, c), hb_fin.reshape(b, c)], axis=1)


def _out_ffn_kernel(x_ref, att_ref, rec_ref, mod_ref, g_ref, wo_ref, wg_ref, wu_ref, wd_ref, *rest, final_norm):
    o_ref = rest[-1]
    mix = jnp.concatenate([att_ref[0], rec_ref[0]], axis=1)
    x1 = x_ref[0] + mod_ref[0, 2:3, :] * jnp.dot(mix, wo_ref[...], preferred_element_type=F32)
    h2 = (_rmsnorm(x1, g_ref[...]) * (1.0 + mod_ref[0, 4:5, :]) + mod_ref[0, 3:4, :]).astype(BF16)
    gate = jnp.dot(h2, wg_ref[...], preferred_element_type=F32)
    up = jnp.dot(h2, wu_ref[...], preferred_element_type=F32)
    act = ((gate * jax.nn.sigmoid(gate)) * up).astype(BF16)
    x2 = x1 + mod_ref[0, 5:6, :] * jnp.dot(act, wd_ref[...], preferred_element_type=F32)
    if final_norm:
        x2 = _rmsnorm(x2, rest[0][...])
    o_ref[0] = x2


def _out_ffn(x, att, rec, mod, g2, w_out, w_gate, w_up, w_down, layer, g_final=None):
    b, t, d = x.shape
    d_ff = w_gate.shape[-1]
    tm = min(256, t)
    shared = mod.shape[0] == 1
    mod_map = (lambda i, j: (0, 0, 0)) if shared else (lambda i, j: (i, 0, 0))
    tok = lambda w: pl.BlockSpec((1, tm, w), lambda i, j: (i, j, 0))
    in_specs = [
        tok(d), tok(ATT_WIDTH), tok(LRU_WIDTH),
        pl.BlockSpec((1, 6, d), mod_map),
        _resident((1, d)),
        _layer_resident((d, d), layer),
        _layer_resident((d, d_ff), layer),
        _layer_resident((d, d_ff), layer),
        _layer_resident((d_ff, d), layer),
    ]
    args = [x, att, rec, mod, g2, w_out, w_gate, w_up, w_down]
    if g_final is not None:
        in_specs.append(_resident((1, d)))
        args.append(g_final)
    return pl.pallas_call(
        functools.partial(_out_ffn_kernel, final_norm=g_final is not None),
        grid=(b, t // tm),
        in_specs=in_specs,
        out_specs=tok(d),
        out_shape=jax.ShapeDtypeStruct((b, t, d), F32),
        compiler_params=_cparams(("parallel", "parallel"), 56),
        name="out_ffn",
    )(*args)


def _block_diag_halves(w):
    per_half = LRU_BLOCKS // 2
    w4 = w.reshape(2, per_half, LRU_BLOCK, LRU_BLOCK)
    eye = jnp.eye(per_half, dtype=w.dtype)
    return jnp.einsum('hnjk,nm->hnjmk', w4, eye).reshape(2, per_half * LRU_BLOCK, per_half * LRU_BLOCK)


def _head_pair_layout(cache):
    b, depth, p = cache.shape[:3]
    return cache.reshape(b, depth, p, HEAD_PAIRS, LANES).transpose(1, 0, 3, 2, 4).astype(BF16)


def kernel(x_prompt, x_sample, cache_k, cache_v, state_lru, c, c_ctx, w_mod, b_mod, norm1, norm2, w_in, w_out, rpb, conv_w, conv_b, lru_a, lru_wr, lru_br, lru_wi, lru_bi, w_gate, w_up, w_down, norm_final):
    depth = w_in.shape[0]
    n_ctx, n_lat = x_prompt.shape[0], x_sample.shape[0]
    d = x_prompt.shape[-1]
    assert n_lat + 1 <= MOD_ROWS

    cvec = jnp.zeros((MOD_ROWS, d), F32).at[:n_lat].set(c).at[n_lat].set(c_ctx)
    mod = _modulation(cvec, w_mod, b_mod).reshape(depth, MOD_ROWS, 6, d)
    t2 = _bias_tables(rpb)
    ck, cv = _head_pair_layout(cache_k), _head_pair_layout(cache_v)

    w_in_b, w_out_b = w_in.astype(BF16), w_out.astype(BF16)
    w_gate_b, w_up_b, w_down_b = w_gate.astype(BF16), w_up.astype(BF16), w_down.astype(BF16)
    wg = jnp.stack([
        jnp.stack([jnp.concatenate([_block_diag_halves(lru_wr[l, dr]), _block_diag_halves(lru_wi[l, dr])], axis=-1)
                   for dr in range(2)]) for l in range(depth)]).astype(BF16)
    bg = 0.5 * jnp.stack([lru_br, lru_bi], axis=2)
    conv_w_half, conv_b_half = 0.5 * conv_w, 0.5 * conv_b
    h0_ctx = jnp.zeros((n_ctx, 2, LRU_WIDTH), F32)

    def layer(x, l, mod_l, attend, h0, last, emit_kv_f32):
        outs = _in_proj(x, mod_l, norm1[l][None], w_in_b, l, emit_kv_f32=emit_kv_f32)
        q, k, v, xb, gy = outs[:5]
        att = attend(q, k, v)
        rec, h_fin = _lru(xb, gy, conv_w_half[l], conv_b_half[l][None], wg[l], bg[l], lru_a[l], h0)
        x = _out_ffn(x, att, rec, mod_l, norm2[l][None], w_out_b, w_gate_b, w_up_b, w_down_b, l,
                     g_final=norm_final[None] if last else None)
        return x, outs[5:], h_fin

    xp, xs = x_prompt, x_sample
    ks_out, vs_out, hs_out = [], [], []
    for l in range(depth):
        last = l == depth - 1
        xp, (k_l, v_l), h_l = layer(xp, l, mod[l, n_lat:n_lat + 1], _ctx_attention, h0_ctx, last, True)
        ks_out.append(k_l.reshape(n_ctx, -1, N_ATT_HEADS, HEAD_DIM))
        vs_out.append(v_l.reshape(n_ctx, -1, N_ATT_HEADS, HEAD_DIM))
        hs_out.append(h_l)
        attend_lat = functools.partial(_lat_attention, kc=ck[l], vc=cv[l], t2=t2[l])
        xs, _, _ = layer(xs, l, mod[l, :n_lat], attend_lat, state_lru[:, l], last, False)
    return (xp, xs, jnp.stack(ks_out, axis=1), jnp.stack(vs_out, axis=1), jnp.stack(hs_out, axis=1))
```

```python
import functools

import jax
import jax.numpy as jnp
from jax import lax
from jax.experimental import pallas as pl
from jax.experimental.pallas import tpu as pltpu

F32 = jnp.float32
BF16 = jnp.bfloat16

D_MODEL = 1024
GRID_W = 64
N_ATT_HEADS = 8
HEAD_DIM = 64
ATT_WIDTH = N_ATT_HEADS * HEAD_DIM
ATT_SCALE = HEAD_DIM ** -0.5
WIN_ROWS = 8
WIN_COLS = 16
LRU_WIDTH = D_MODEL // 2
LRU_BLOCKS = 8
LRU_BLOCK = LRU_WIDTH // LRU_BLOCKS
CONV_W = 4
CONV_LEFT = 2
LRU_C = 8.0
IN_COLS = 3 * ATT_WIDTH + 2 * LRU_WIDTH
EPS = 1e-6
NEG_INF = -1e30
F32_TINY = float(jnp.finfo(jnp.float32).tiny)
LOG2_E = 1.4426950408889634

LANES = 128
SUBLANES = 8
HEAD_PAIRS = ATT_WIDTH // LANES
N_ROW_IDX = 2 * WIN_ROWS - 1
N_COL_IDX = 2 * WIN_COLS - 1
N_BIAS_TILES = N_ROW_IDX - 1
MOD_ROWS = 16
LRU_GROUP = SUBLANES
SCAN_PAD = SUBLANES
X_PAD = 3 * SUBLANES
LRU_TIME_BLOCK = 32
MIB = 1024 * 1024


def _cparams(semantics, vmem_mib):
    return pltpu.CompilerParams(dimension_semantics=semantics, vmem_limit_bytes=vmem_mib * MIB)


def _resident(shape):
    zeros = (0,) * len(shape)
    return pl.BlockSpec(shape, lambda *_: zeros, pipeline_mode=pl.Buffered(1))


def _layer_resident(shape, layer):
    index = (layer,) + (0,) * len(shape)
    return pl.BlockSpec((None,) + tuple(shape), lambda *_: index, pipeline_mode=pl.Buffered(1))


def _rmsnorm(x, g):
    return (x * lax.rsqrt(jnp.mean(x * x, axis=-1, keepdims=True) + EPS)) * g


def _mod_kernel(c_ref, w_ref, b_ref, o_ref):
    cv = c_ref[...]
    s = (cv * jax.nn.sigmoid(cv)).astype(BF16)
    o_ref[0] = jnp.dot(s, w_ref[0].astype(BF16), preferred_element_type=F32) + b_ref[0]


def _modulation(cvec, w_mod, b_mod):
    depth, d, n = w_mod.shape
    tn = n // 4
    return pl.pallas_call(
        _mod_kernel,
        grid=(depth, n // tn),
        in_specs=[
            pl.BlockSpec((MOD_ROWS, d), lambda l, j: (0, 0)),
            pl.BlockSpec((1, d, tn), lambda l, j: (l, 0, j)),
            pl.BlockSpec((1, 1, tn), lambda l, j: (l, 0, j)),
        ],
        out_specs=pl.BlockSpec((1, MOD_ROWS, tn), lambda l, j: (l, 0, j)),
        out_shape=jax.ShapeDtypeStruct((depth, MOD_ROWS, n), F32),
        compiler_params=_cparams(("arbitrary", "arbitrary"), 40),
        name="adaln_mod",
    )(cvec, w_mod, b_mod.reshape(depth, 1, n))


def _bias_table_kernel(rpb_ref, o_ref):
    shape = (GRID_W, LANES)
    c = lax.broadcasted_iota(jnp.int32, shape, 0)
    lane = lax.broadcasted_iota(jnp.int32, shape, 1)
    upper = lane >= GRID_W
    kc = jnp.where(upper, lane - GRID_W, lane)
    cs = jnp.clip(c - WIN_COLS // 2, 0, GRID_W - WIN_COLS)
    ok = (kc >= cs) & (kc < cs + WIN_COLS)
    lo_shift = LANES - (WIN_COLS - 1)
    hi_shift = (lo_shift + GRID_W) % LANES
    for h in range(2):
        for i in range(N_BIAS_TILES):
            lo = pltpu.roll(jnp.broadcast_to(rpb_ref[0, h, i:i + 1, :], shape), lo_shift, 1, stride=1, stride_axis=0)
            hi = pltpu.roll(jnp.broadcast_to(rpb_ref[0, h, i + 1:i + 2, :], shape), hi_shift, 1, stride=1, stride_axis=0)
            o_ref[0, i, h * GRID_W:(h + 1) * GRID_W, :] = jnp.where(ok, jnp.where(upper, hi, lo), NEG_INF)


def _bias_tables(rpb):
    depth = rpb.shape[0]
    n = depth * HEAD_PAIRS
    padded = jnp.pad(rpb, ((0, 0), (0, 0), (0, 2 * SUBLANES - N_ROW_IDX), (0, LANES - N_COL_IDX)))
    out = pl.pallas_call(
        _bias_table_kernel,
        grid=(n,),
        in_specs=[pl.BlockSpec((1, 2, 2 * SUBLANES, LANES), lambda i: (i, 0, 0, 0))],
        out_specs=pl.BlockSpec((1, N_BIAS_TILES, LANES, LANES), lambda i: (i, 0, 0, 0)),
        out_shape=jax.ShapeDtypeStruct((n, N_BIAS_TILES, LANES, LANES), F32),
        compiler_params=_cparams(("arbitrary",), 16),
        name="bias_tables",
    )(padded.reshape(n, 2, 2 * SUBLANES, LANES))
    return out.reshape(depth, HEAD_PAIRS, N_BIAS_TILES, LANES, LANES)


def _in_proj_kernel(x_ref, mod_ref, g_ref, w_ref, q_ref, k_ref, v_ref, xb_ref, gy_ref, *kv_f32_refs):
    h = _rmsnorm(x_ref[0], g_ref[...]) * (1.0 + mod_ref[0, 1:2, :]) + mod_ref[0, 0:1, :]
    p = jnp.dot(h.astype(BF16), w_ref[...], preferred_element_type=F32)
    for j in range(HEAD_PAIRS):
        lo = j * LANES
        q_ref[0, j] = (p[:, lo:lo + LANES] * ATT_SCALE).astype(BF16)
        k_ref[0, j] = p[:, ATT_WIDTH + lo:ATT_WIDTH + lo + LANES].astype(BF16)
        v_ref[0, j] = p[:, 2 * ATT_WIDTH + lo:2 * ATT_WIDTH + lo + LANES].astype(BF16)
    xb_ref[0] = p[:, 3 * ATT_WIDTH:3 * ATT_WIDTH + LRU_WIDTH]
    gy_ref[0] = jax.nn.gelu(p[:, 3 * ATT_WIDTH + LRU_WIDTH:])
    if kv_f32_refs:
        kf_ref, vf_ref = kv_f32_refs
        kf_ref[0] = p[:, ATT_WIDTH:2 * ATT_WIDTH]
        vf_ref[0] = p[:, 2 * ATT_WIDTH:3 * ATT_WIDTH]


def _in_proj(x, mod, g, w, layer, *, emit_kv_f32):
    b, t, d = x.shape
    tm = min(1024, t)
    shared = mod.shape[0] == 1
    mod_map = (lambda i, j: (0, 0, 0)) if shared else (lambda i, j: (i, 0, 0))
    heads_spec = pl.BlockSpec((1, HEAD_PAIRS, tm, LANES), lambda i, j: (i, 0, j, 0))
    heads_shape = jax.ShapeDtypeStruct((b, HEAD_PAIRS, t, LANES), BF16)
    half_spec = pl.BlockSpec((1, tm, LRU_WIDTH), lambda i, j: (i, j, 0))
    half_shape = jax.ShapeDtypeStruct((b, t, LRU_WIDTH), F32)
    n_f32 = 4 if emit_kv_f32 else 2
    return pl.pallas_call(
        _in_proj_kernel,
        grid=(b, t // tm),
        in_specs=[
            pl.BlockSpec((1, tm, d), lambda i, j: (i, j, 0)),
            pl.BlockSpec((1, 6, d), mod_map),
            _resident((1, d)),
            _layer_resident((d, IN_COLS), layer),
        ],
        out_specs=[heads_spec] * 3 + [half_spec] * n_f32,
        out_shape=[heads_shape] * 3 + [half_shape] * n_f32,
        compiler_params=_cparams(("parallel", "parallel"), 48),
        name="in_proj",
    )(x, mod, g, w)


def _head_masks(shape):
    lane = lax.broadcasted_iota(jnp.int32, shape, 1)
    return lane < HEAD_DIM, lane >= HEAD_DIM


def _scores(qh, k):
    return lax.dot_general(qh, k, (((1,), (1,)), ((), ())), preferred_element_type=F32)


def _ctx_attn_kernel(q_ref, k_ref, v_ref, o_ref):
    t = q_ref.shape[2]
    own_lanes = ((lax.broadcasted_iota(jnp.int32, (2 * t, LANES), 0) < t)
                 == (lax.broadcasted_iota(jnp.int32, (2 * t, LANES), 1) < HEAD_DIM))
    lo_lanes, _ = _head_masks((t, LANES))
    scores = []
    for hp in range(HEAD_PAIRS):
        q2 = jnp.concatenate([q_ref[0, hp], q_ref[0, hp]], axis=0)
        scores.append(_scores(jnp.where(own_lanes, q2, jnp.zeros_like(q2)), k_ref[0, hp]))
    for hp, s in enumerate(scores):
        e = jnp.exp(s - jnp.max(s, axis=-1, keepdims=True))
        l = jnp.sum(e, axis=-1, keepdims=True)
        o2 = jnp.dot(e.astype(BF16), v_ref[0, hp], preferred_element_type=F32) / l
        o_ref[0, :, hp * LANES:(hp + 1) * LANES] = jnp.where(lo_lanes, o2[:t], o2[t:]).astype(BF16)


def _ctx_attention(q, k, v):
    b, _, t, _ = q.shape
    spec = pl.BlockSpec((1, HEAD_PAIRS, t, LANES), lambda i: (i, 0, 0, 0))
    return pl.pallas_call(
        _ctx_attn_kernel,
        grid=(b,),
        in_specs=[spec] * 3,
        out_specs=pl.BlockSpec((1, t, ATT_WIDTH), lambda i: (i, 0, 0)),
        out_shape=jax.ShapeDtypeStruct((b, t, ATT_WIDTH), BF16),
        compiler_params=_cparams(("parallel",), 32),
        name="ctx_attention",
    )(q, k, v)


def _lat_attn_kernel(q_ref, k_ref, v_ref, kc_ref, vc_ref, t2_ref, o_ref, *, rows_per_step, n_rows):
    rb = pl.program_id(2)
    kc, vc = kc_ref[0, 0], vc_ref[0, 0]
    n_loc = WIN_ROWS * GRID_W
    shape2 = (2 * GRID_W, LANES)
    own_lanes = ((lax.broadcasted_iota(jnp.int32, shape2, 0) < GRID_W)
                 == (lax.broadcasted_iota(jnp.int32, shape2, 1) < HEAD_DIM))
    lo_lanes, _ = _head_masks((GRID_W, LANES))

    def scores(i):
        r = rb * rows_per_step + i
        rs = jnp.clip(r - WIN_ROWS // 2, 0, n_rows - WIN_ROWS)
        off = rs - r + WIN_ROWS - 1
        q = q_ref[0, 0, i * GRID_W:(i + 1) * GRID_W, :]
        q2 = jnp.concatenate([q, q], axis=0)
        q2 = jnp.where(own_lanes, q2, jnp.zeros_like(q2))
        k_start = pl.multiple_of(rs * GRID_W, GRID_W)
        bias = jnp.concatenate([t2_ref[0, off + 2 * j] for j in range(WIN_ROWS // 2)], axis=1)
        s_loc = _scores(q2, k_ref[0, 0, pl.ds(k_start, n_loc), :]) + bias
        s_ctx = _scores(q2, kc)
        return jnp.concatenate([s_loc, s_ctx], axis=1), k_start

    def finish(i, s, m, k_start):
        e = jnp.exp(s - m)
        l = jnp.sum(e, axis=-1, keepdims=True)
        e = e.astype(BF16)
        o2 = (jnp.dot(e[:, :n_loc], v_ref[0, 0, pl.ds(k_start, n_loc), :], preferred_element_type=F32)
              + jnp.dot(e[:, n_loc:], vc, preferred_element_type=F32)) / l
        o_ref[0, i * GRID_W:(i + 1) * GRID_W, :] = jnp.where(lo_lanes, o2[:GRID_W], o2[GRID_W:]).astype(BF16)

    rows = [scores(0), scores(1)] + [None] * rows_per_step
    maxima = [jnp.max(rows[0][0], axis=-1, keepdims=True)] + [None] * rows_per_step
    for i in range(rows_per_step):
        if i + 2 < rows_per_step:
            rows[i + 2] = scores(i + 2)
        if i + 1 < rows_per_step:
            maxima[i + 1] = jnp.max(rows[i + 1][0], axis=-1, keepdims=True)
        finish(i, rows[i][0], maxima[i], rows[i][1])
        rows[i] = maxima[i] = None


def _lat_attention(q, k, v, kc, vc, t2):
    b, _, t, _ = q.shape
    lc = kc.shape[2]
    n_rows = t // GRID_W
    rows_per_step = min(32, n_rows)
    tq = rows_per_step * GRID_W
    full = pl.BlockSpec((1, 1, t, LANES), lambda i, j, r: (i, j, 0, 0))
    ctx = pl.BlockSpec((1, 1, lc, LANES), lambda i, j, r: (i, j, 0, 0))
    return pl.pallas_call(
        functools.partial(_lat_attn_kernel, rows_per_step=rows_per_step, n_rows=n_rows),
        grid=(b, HEAD_PAIRS, n_rows // rows_per_step),
        in_specs=[
            pl.BlockSpec((1, 1, tq, LANES), lambda i, j, r: (i, j, r, 0)),
            full, full, ctx, ctx,
            pl.BlockSpec((1, N_BIAS_TILES, LANES, LANES), lambda i, j, r: (j, 0, 0, 0)),
        ],
        out_specs=pl.BlockSpec((1, tq, LANES), lambda i, j, r: (i, r, j)),
        out_shape=jax.ShapeDtypeStruct((b, t, ATT_WIDTH), BF16),
        compiler_params=_cparams(("parallel", "parallel", "arbitrary"), 32),
        name="lat_attention",
    )(q, k, v, kc, vc, t2)


def _log_sigmoid(x):
    return jnp.minimum(x, 0.0) - jnp.log1p(jnp.exp(-jnp.abs(x)))


def _gate_matmuls(xh, wg_ref):
    half = LRU_WIDTH // 2
    xhb = xh.astype(BF16)
    return [jnp.dot(xhb[:, hf * half:(hf + 1) * half], wg_ref[hf], preferred_element_type=F32) for hf in range(2)]


def _lru_coefficients(pre, xh, bg_ref, neg_log_sig4, neg_log2_sig4):
    half = LRU_WIDTH // 2
    t_r = jnp.tanh(jnp.concatenate([pre[0][:, :half], pre[1][:, :half]], axis=1) + bg_ref[0:1, :])
    t_i = jnp.tanh(jnp.concatenate([pre[0][:, half:], pre[1][:, half:]], axis=1) + bg_ref[1:2, :])
    gate = t_r + 1.0
    a = jnp.exp2(gate * neg_log2_sig4)
    z = jnp.tanh(gate * neg_log_sig4) * (1.0 + a * a)
    root = z * lax.rsqrt(jnp.maximum(z, F32_TINY))
    return a, (root * (t_i + 1.0)) * xh


def _pipelined_coefficients(n_blocks, gate_inputs, xh_block, bg_ref, scales, a_s, b_s):
    m = LRU_TIME_BLOCK * LRU_GROUP
    pre = gate_inputs(0)
    for blk in range(n_blocks):
        following = gate_inputs(blk + 1) if blk + 1 < n_blocks else None
        rows = slice(blk * m, (blk + 1) * m)
        a_s[rows, :], b_s[rows, :] = _lru_coefficients(pre, xh_block(rows), bg_ref, *scales)
        pre = following


def _log_sigmoid_scales(la_ref):
    neg_log_sig4 = (-0.5 * LRU_C) * _log_sigmoid(la_ref[...])
    return neg_log_sig4, neg_log_sig4 * (-LOG2_E)


def _lru_bwd_kernel(xb_ref, xprev_ref, xnext_ref, cw_ref, cb_ref, wg_ref, bg_ref, la_ref, h0_ref,
                    xh_ref, hb_ref, hfin_ref, x_s, a_s, b_s, carry_s, *, tc):
    j = pl.program_id(1)
    n = pl.num_programs(1)
    cj = n - 1 - j
    x_pitch = tc + X_PAD
    n_slabs = LRU_WIDTH // LANES
    lane_slab = lambda v, c: v[:, c * LANES:(c + 1) * LANES]

    @pl.when(j == 0)
    def _():
        carry_s[...] = h0_ref[0]

    def park(bi, carry):
        base = pl.multiple_of(bi * x_pitch, SUBLANES)
        pieces = ((0, jnp.where(cj > 0, xprev_ref[bi], 0.0)), (SUBLANES, xb_ref[bi]),
                  (SUBLANES + tc, jnp.where(cj < n - 1, xnext_ref[bi], 0.0)))
        for start, v in pieces:
            for c in range(n_slabs):
                x_s[c, pl.ds(base + start, v.shape[0]), :] = lane_slab(v, c)
        return carry

    lax.fori_loop(0, LRU_GROUP, park, 0)

    scales = _log_sigmoid_scales(la_ref)

    def time_rows(t):
        rows = pl.ds(SUBLANES + t, LRU_GROUP, stride=x_pitch)
        return jnp.concatenate([x_s[c, rows, :] for c in range(n_slabs)], axis=1)

    m = LRU_TIME_BLOCK * LRU_GROUP

    def conv_block(blk):
        t0 = blk * LRU_TIME_BLOCK
        xw = jnp.concatenate([time_rows(t0 + k) for k in range(-CONV_LEFT, LRU_TIME_BLOCK + CONV_W - 1 - CONV_LEFT)],
                             axis=0)
        xh = cb_ref[...]
        for tap in range(CONV_W):
            xh = xh + cw_ref[tap:tap + 1, :] * xw[tap * LRU_GROUP:tap * LRU_GROUP + m]
        xh_ref[0, blk * m:(blk + 1) * m, :] = xh
        return _gate_matmuls(xh, wg_ref)

    _pipelined_coefficients(tc // LRU_TIME_BLOCK, conv_block, lambda rows: xh_ref[0, rows, :], bg_ref, scales, a_s, b_s)

    def step(i, h):
        rows = pl.ds(pl.multiple_of((tc - 1 - i) * LRU_GROUP, LRU_GROUP), LRU_GROUP)
        h = a_s[rows, :] * h + b_s[rows, :]
        hb_ref[0, rows, :] = h
        return h

    carry_s[...] = lax.fori_loop(0, tc, step, carry_s[...], unroll=8)

    @pl.when(j == n - 1)
    def _():
        hfin_ref[0] = carry_s[...]


def _lru_fwd_kernel(xh_ref, hb_ref, gy_ref, wg_ref, bg_ref, la_ref, h0_ref, out_ref, hfin_ref,
                    a_s, b_s, s_s, carry_s, *, tc):
    j = pl.program_id(1)
    n = pl.num_programs(1)
    pitch = tc + SCAN_PAD
    n_slabs = LRU_WIDTH // LANES
    m = LRU_TIME_BLOCK * LRU_GROUP

    @pl.when(j == 0)
    def _():
        carry_s[...] = h0_ref[0]

    scales = _log_sigmoid_scales(la_ref)

    xh_block = lambda rows: xh_ref[0, rows, :]
    _pipelined_coefficients(tc // LRU_TIME_BLOCK, lambda blk: _gate_matmuls(xh_block(slice(blk * m, (blk + 1) * m)), wg_ref),
                            xh_block, bg_ref, scales, a_s, b_s)

    def step(t, h):
        rows = pl.ds(pl.multiple_of(t * LRU_GROUP, LRU_GROUP), LRU_GROUP)
        h = a_s[rows, :] * h + b_s[rows, :]
        both = h + hb_ref[0, rows, :]
        for c in range(n_slabs):
            s_s[c, pl.ds(t, LRU_GROUP, stride=pitch), :] = both[:, c * LANES:(c + 1) * LANES]
        return h

    carry_s[...] = lax.fori_loop(0, tc, step, carry_s[...], unroll=8)

    def emit(bi, carry):
        rows = pl.ds(pl.multiple_of(bi * pitch, SUBLANES), tc)
        both = jnp.concatenate([s_s[c, rows, :] for c in range(n_slabs)], axis=1)
        out_ref[bi] = (both * gy_ref[bi]).astype(BF16)
        return carry

    lax.fori_loop(0, LRU_GROUP, emit, 0)

    @pl.when(j == n - 1)
    def _():
        hfin_ref[0] = carry_s[...]


def _lru(xb, gy, conv_w, conv_b, wg, bg, la, h0):
    b, t, c = xb.shape
    assert b % LRU_GROUP == 0
    groups = b // LRU_GROUP
    tc = min(256, t)
    n = t // tc
    per = tc // SUBLANES
    assert tc % (2 * SUBLANES) == 0 and tc % LRU_TIME_BLOCK == 0
    h0 = h0.reshape(groups, LRU_GROUP, 2, c)
    state = pl.BlockSpec((1, LRU_GROUP, c), lambda g, j: (g, 0, 0))
    state_shape = jax.ShapeDtypeStruct((groups, LRU_GROUP, c), F32)
    tm_rows = tc * LRU_GROUP
    tm_shape = jax.ShapeDtypeStruct((groups, t * LRU_GROUP, c), F32)
    slabs = lambda pad: pltpu.VMEM((c // LANES, LRU_GROUP * (tc + pad), LANES), F32)
    tm_scratch = pltpu.VMEM((tm_rows, c), F32)
    carry = pltpu.VMEM((LRU_GROUP, c), F32)
    gate_specs = [_resident((2, c // 2, c)), _resident((2, c)), _resident((1, c)), state]

    rev = lambda j: n - 1 - j
    rev_tm = pl.BlockSpec((1, tm_rows, c), lambda g, j: (g, rev(j), 0))
    halo = lambda f: pl.BlockSpec((LRU_GROUP, SUBLANES, c), lambda g, j: (g, f(rev(j)), 0))
    xh, hb, hb_fin = pl.pallas_call(
        functools.partial(_lru_bwd_kernel, tc=tc),
        grid=(groups, n),
        in_specs=[
            pl.BlockSpec((LRU_GROUP, tc, c), lambda g, j: (g, rev(j), 0)),
            halo(lambda cj: jnp.maximum(cj * per - 1, 0)),
            halo(lambda cj: jnp.minimum((cj + 1) * per, t // SUBLANES - 1)),
            _resident((CONV_W, c)),
            _resident((1, c)),
        ] + gate_specs,
        out_specs=[rev_tm, rev_tm, state],
        out_shape=[tm_shape, tm_shape, state_shape],
        scratch_shapes=[slabs(X_PAD), tm_scratch, tm_scratch, carry],
        compiler_params=_cparams(("parallel", "arbitrary"), 56),
        name="lru_bwd",
    )(xb, xb, xb, conv_w, conv_b, wg[1], bg[1], la[1:2], h0[:, :, 1])

    fwd_tm = pl.BlockSpec((1, tm_rows, c), lambda g, j: (g, j, 0))
    chunk = pl.BlockSpec((LRU_GROUP, tc, c), lambda g, j: (g, j, 0))
    rec, hf_fin = pl.pallas_call(
        functools.partial(_lru_fwd_kernel, tc=tc),
        grid=(groups, n),
        in_specs=[fwd_tm, fwd_tm, chunk] + gate_specs,
        out_specs=[chunk, state],
        out_shape=[jax.ShapeDtypeStruct((b, t, c), BF16), state_shape],
        scratch_shapes=[tm_scratch, tm_scratch, slabs(SCAN_PAD), carry],
        compiler_params=_cparams(("parallel", "arbitrary"), 56),
        name="lru_fwd",
    )(xh, hb, gy, wg[0], bg[0], la[0:1], h0[:, :, 0])
    return rec, jnp.stack([hf_fin.reshape(b, c), hb_fin.reshape(b, c)], axis=1)


def _out_ffn_kernel(x_ref, att_ref, rec_ref, mod_ref, g_ref, wo_ref, wg_ref, wu_ref, wd_ref, *rest, final_norm):
    o_ref = rest[-1]
    mix = jnp.concatenate([att_ref[0], rec_ref[0]], axis=1)
    x1 = x_ref[0] + mod_ref[0, 2:3, :] * jnp.dot(mix, wo_ref[...], preferred_element_type=F32)
    h2 = (_rmsnorm(x1, g_ref[...]) * (1.0 + mod_ref[0, 4:5, :]) + mod_ref[0, 3:4, :]).astype(BF16)
    gate = jnp.dot(h2, wg_ref[...], preferred_element_type=F32)
    up = jnp.dot(h2, wu_ref[...], preferred_element_type=F32)
    act = ((gate * jax.nn.sigmoid(gate)) * up).astype(BF16)
    x2 = x1 + mod_ref[0, 5:6, :] * jnp.dot(act, wd_ref[...], preferred_element_type=F32)
    if final_norm:
        x2 = _rmsnorm(x2, rest[0][...])
    o_ref[0] = x2


def _out_ffn(x, att, rec, mod, g2, w_out, w_gate, w_up, w_down, layer, g_final=None):
    b, t, d = x.shape
    d_ff = w_gate.shape[-1]
    tm = min(512, t)
    shared = mod.shape[0] == 1
    mod_map = (lambda i, j: (0, 0, 0)) if shared else (lambda i, j: (i, 0, 0))
    tok = lambda w: pl.BlockSpec((1, tm, w), lambda i, j: (i, j, 0))
    in_specs = [
        tok(d), tok(ATT_WIDTH), tok(LRU_WIDTH),
        pl.BlockSpec((1, 6, d), mod_map),
        _resident((1, d)),
        _layer_resident((d, d), layer),
        _layer_resident((d, d_ff), layer),
        _layer_resident((d, d_ff), layer),
        _layer_resident((d_ff, d), layer),
    ]
    args = [x, att, rec, mod, g2, w_out, w_gate, w_up, w_down]
    if g_final is not None:
        in_specs.append(_resident((1, d)))
        args.append(g_final)
    return pl.pallas_call(
        functools.partial(_out_ffn_kernel, final_norm=g_final is not None),
        grid=(b, t // tm),
        in_specs=in_specs,
        out_specs=tok(d),
        out_shape=jax.ShapeDtypeStruct((b, t, d), F32),
        compiler_params=_cparams(("parallel", "parallel"), 56),
        name="out_ffn",
    )(*args)


def _block_diag_halves(w):
    per_half = LRU_BLOCKS // 2
    w4 = w.reshape(2, per_half, LRU_BLOCK, LRU_BLOCK)
    eye = jnp.eye(per_half, dtype=w.dtype)
    return jnp.einsum('hnjk,nm->hnjmk', w4, eye).reshape(2, per_half * LRU_BLOCK, per_half * LRU_BLOCK)


def _head_pair_layout(cache):
    b, depth, p = cache.shape[:3]
    return cache.reshape(b, depth, p, HEAD_PAIRS, LANES).transpose(1, 0, 3, 2, 4).astype(BF16)


def kernel(x_prompt, x_sample, cache_k, cache_v, state_lru, c, c_ctx, w_mod, b_mod, norm1, norm2, w_in, w_out, rpb, conv_w, conv_b, lru_a, lru_wr, lru_br, lru_wi, lru_bi, w_gate, w_up, w_down, norm_final):
    depth = w_in.shape[0]
    n_ctx, n_lat = x_prompt.shape[0], x_sample.shape[0]
    d = x_prompt.shape[-1]
    assert n_lat + 1 <= MOD_ROWS

    cvec = jnp.zeros((MOD_ROWS, d), F32).at[:n_lat].set(c).at[n_lat].set(c_ctx)
    mod = _modulation(cvec, w_mod, b_mod).reshape(depth, MOD_ROWS, 6, d)
    t2 = _bias_tables(rpb)
    ck, cv = _head_pair_layout(cache_k), _head_pair_layout(cache_v)

    w_in_b, w_out_b = w_in.astype(BF16), w_out.astype(BF16)
    w_gate_b, w_up_b, w_down_b = w_gate.astype(BF16), w_up.astype(BF16), w_down.astype(BF16)
    wg = jnp.stack([
        jnp.stack([jnp.concatenate([_block_diag_halves(lru_wr[l, dr]), _block_diag_halves(lru_wi[l, dr])], axis=-1)
                   for dr in range(2)]) for l in range(depth)]).astype(BF16)
    bg = 0.5 * jnp.stack([lru_br, lru_bi], axis=2)
    conv_w_half, conv_b_half = 0.5 * conv_w, 0.5 * conv_b
    h0_ctx = jnp.zeros((n_ctx, 2, LRU_WIDTH), F32)

    def layer(x, l, mod_l, attend, h0, last, emit_kv_f32):
        outs = _in_proj(x, mod_l, norm1[l][None], w_in_b, l, emit_kv_f32=emit_kv_f32)
        q, k, v, xb, gy = outs[:5]
        att = attend(q, k, v)
        rec, h_fin = _lru(xb, gy, conv_w_half[l], conv_b_half[l][None], wg[l], bg[l], lru_a[l], h0)
        x = _out_ffn(x, att, rec, mod_l, norm2[l][None], w_out_b, w_gate_b, w_up_b, w_down_b, l,
                     g_final=norm_final[None] if last else None)
        return x, outs[5:], h_fin

    xp, xs = x_prompt, x_sample
    ks_out, vs_out, hs_out = [], [], []
    for l in range(depth):
        last = l == depth - 1
        xp, (k_l, v_l), h_l = layer(xp, l, mod[l, n_lat:n_lat + 1], _ctx_attention, h0_ctx, last, True)
        ks_out.append(k_l.reshape(n_ctx, -1, N_ATT_HEADS, HEAD_DIM))
        vs_out.append(v_l.reshape(n_ctx, -1, N_ATT_HEADS, HEAD_DIM))
        hs_out.append(h_l)
        attend_lat = functools.partial(_lat_attention, kc=ck[l], vc=cv[l], t2=t2[l])
        xs, _, _ = layer(xs, l, mod[l, :n_lat], attend_lat, state_lru[:, l], last, False)
    return (xp, xs, jnp.stack(ks_out, axis=1), jnp.stack(vs_out, axis=1), jnp.stack(hs_out, axis=1))
```

```python
import functools

import jax
import jax.numpy as jnp
from jax import lax
from jax.experimental import pallas as pl
from jax.experimental.pallas import tpu as pltpu

F32 = jnp.float32
BF16 = jnp.bfloat16

D_MODEL = 1024
GRID_W = 64
N_ATT_HEADS = 8
HEAD_DIM = 64
ATT_WIDTH = N_ATT_HEADS * HEAD_DIM
ATT_SCALE = HEAD_DIM ** -0.5
WIN_ROWS = 8
WIN_COLS = 16
LRU_WIDTH = D_MODEL // 2
LRU_BLOCKS = 8
LRU_BLOCK = LRU_WIDTH // LRU_BLOCKS
CONV_W = 4
CONV_LEFT = 2
LRU_C = 8.0
IN_COLS = 3 * ATT_WIDTH + 2 * LRU_WIDTH
EPS = 1e-6
NEG_INF = -1e30
F32_TINY = float(jnp.finfo(jnp.float32).tiny)
LOG2_E = 1.4426950408889634

LANES = 128
SUBLANES = 8
HEAD_PAIRS = ATT_WIDTH // LANES
N_ROW_IDX = 2 * WIN_ROWS - 1
N_COL_IDX = 2 * WIN_COLS - 1
N_BIAS_TILES = N_ROW_IDX - 1
MOD_ROWS = 16
LRU_GROUP = SUBLANES
SCAN_PAD = SUBLANES
X_PAD = 3 * SUBLANES
LRU_TIME_BLOCK = 32
MIB = 1024 * 1024


def _cparams(semantics, vmem_mib):
    return pltpu.CompilerParams(dimension_semantics=semantics, vmem_limit_bytes=vmem_mib * MIB)


def _resident(shape):
    zeros = (0,) * len(shape)
    return pl.BlockSpec(shape, lambda *_: zeros, pipeline_mode=pl.Buffered(1))


def _layer_resident(shape, layer):
    index = (layer,) + (0,) * len(shape)
    return pl.BlockSpec((None,) + tuple(shape), lambda *_: index, pipeline_mode=pl.Buffered(1))


def _rmsnorm(x, g):
    return (x * lax.rsqrt(jnp.mean(x * x, axis=-1, keepdims=True) + EPS)) * g


def _mod_kernel(c_ref, w_ref, b_ref, o_ref):
    cv = c_ref[...]
    s = (cv * jax.nn.sigmoid(cv)).astype(BF16)
    o_ref[0] = jnp.dot(s, w_ref[0].astype(BF16), preferred_element_type=F32) + b_ref[0]


def _modulation(cvec, w_mod, b_mod):
    depth, d, n = w_mod.shape
    tn = n // 4
    return pl.pallas_call(
        _mod_kernel,
        grid=(depth, n // tn),
        in_specs=[
            pl.BlockSpec((MOD_ROWS, d), lambda l, j: (0, 0)),
            pl.BlockSpec((1, d, tn), lambda l, j: (l, 0, j)),
            pl.BlockSpec((1, 1, tn), lambda l, j: (l, 0, j)),
        ],
        out_specs=pl.BlockSpec((1, MOD_ROWS, tn), lambda l, j: (l, 0, j)),
        out_shape=jax.ShapeDtypeStruct((depth, MOD_ROWS, n), F32),
        compiler_params=_cparams(("arbitrary", "arbitrary"), 40),
        name="adaln_mod",
    )(cvec, w_mod, b_mod.reshape(depth, 1, n))


def _bias_table_kernel(rpb_ref, o_ref):
    shape = (GRID_W, LANES)
    c = lax.broadcasted_iota(jnp.int32, shape, 0)
    lane = lax.broadcasted_iota(jnp.int32, shape, 1)
    upper = lane >= GRID_W
    kc = jnp.where(upper, lane - GRID_W, lane)
    cs = jnp.clip(c - WIN_COLS // 2, 0, GRID_W - WIN_COLS)
    ok = (kc >= cs) & (kc < cs + WIN_COLS)
    lo_shift = LANES - (WIN_COLS - 1)
    hi_shift = (lo_shift + GRID_W) % LANES
    for h in range(2):
        for i in range(N_BIAS_TILES):
            lo = pltpu.roll(jnp.broadcast_to(rpb_ref[0, h, i:i + 1, :], shape), lo_shift, 1, stride=1, stride_axis=0)
            hi = pltpu.roll(jnp.broadcast_to(rpb_ref[0, h, i + 1:i + 2, :], shape), hi_shift, 1, stride=1, stride_axis=0)
            o_ref[0, i, h * GRID_W:(h + 1) * GRID_W, :] = jnp.where(ok, jnp.where(upper, hi, lo), NEG_INF)


def _bias_tables(rpb):
    depth = rpb.shape[0]
    n = depth * HEAD_PAIRS
    padded = jnp.pad(rpb, ((0, 0), (0, 0), (0, 2 * SUBLANES - N_ROW_IDX), (0, LANES - N_COL_IDX)))
    out = pl.pallas_call(
        _bias_table_kernel,
        grid=(n,),
        in_specs=[pl.BlockSpec((1, 2, 2 * SUBLANES, LANES), lambda i: (i, 0, 0, 0))],
        out_specs=pl.BlockSpec((1, N_BIAS_TILES, LANES, LANES), lambda i: (i, 0, 0, 0)),
        out_shape=jax.ShapeDtypeStruct((n, N_BIAS_TILES, LANES, LANES), F32),
        compiler_params=_cparams(("arbitrary",), 16),
        name="bias_tables",
    )(padded.reshape(n, 2, 2 * SUBLANES, LANES))
    return out.reshape(depth, HEAD_PAIRS, N_BIAS_TILES, LANES, LANES)


def _in_proj_kernel(x_ref, mod_ref, g_ref, w_ref, q_ref, k_ref, v_ref, xb_ref, gy_ref, *kv_f32_refs):
    h = _rmsnorm(x_ref[0], g_ref[...]) * (1.0 + mod_ref[0, 1:2, :]) + mod_ref[0, 0:1, :]
    p = jnp.dot(h.astype(BF16), w_ref[...], preferred_element_type=F32)
    for j in range(HEAD_PAIRS):
        lo = j * LANES
        q_ref[0, j] = (p[:, lo:lo + LANES] * ATT_SCALE).astype(BF16)
        k_ref[0, j] = p[:, ATT_WIDTH + lo:ATT_WIDTH + lo + LANES].astype(BF16)
        v_ref[0, j] = p[:, 2 * ATT_WIDTH + lo:2 * ATT_WIDTH + lo + LANES].astype(BF16)
    xb_ref[0] = p[:, 3 * ATT_WIDTH:3 * ATT_WIDTH + LRU_WIDTH]
    gy_ref[0] = jax.nn.gelu(p[:, 3 * ATT_WIDTH + LRU_WIDTH:])
    if kv_f32_refs:
        kf_ref, vf_ref = kv_f32_refs
        kf_ref[0] = p[:, ATT_WIDTH:2 * ATT_WIDTH]
        vf_ref[0] = p[:, 2 * ATT_WIDTH:3 * ATT_WIDTH]


def _in_proj_kv_kernel(x_ref, mod_ref, g_ref, w_ref, k_all_ref, v_all_ref, *out_refs):
    del k_all_ref, v_all_ref
    _in_proj_kernel(x_ref, mod_ref, g_ref, w_ref, *out_refs)


def _in_proj(x, mod, g, w, layer, kv_all=None):
    b, t, d = x.shape
    tm = min(1024, t)
    shared = mod.shape[0] == 1
    mod_map = (lambda i, j: (0, 0, 0)) if shared else (lambda i, j: (i, 0, 0))
    heads_spec = pl.BlockSpec((1, HEAD_PAIRS, tm, LANES), lambda i, j: (i, 0, j, 0))
    heads_shape = jax.ShapeDtypeStruct((b, HEAD_PAIRS, t, LANES), BF16)
    half_spec = pl.BlockSpec((1, tm, LRU_WIDTH), lambda i, j: (i, j, 0))
    half_shape = jax.ShapeDtypeStruct((b, t, LRU_WIDTH), F32)
    in_specs = [
        pl.BlockSpec((1, tm, d), lambda i, j: (i, j, 0)),
        pl.BlockSpec((1, 6, d), mod_map),
        _resident((1, d)),
        _layer_resident((d, IN_COLS), layer),
    ]
    out_specs = [heads_spec] * 3 + [half_spec] * 2
    out_shape = [heads_shape] * 3 + [half_shape] * 2
    args = [x, mod, g, w]
    aliases = {}
    if kv_all is not None:
        layer_spec = pl.BlockSpec((1, None, tm, ATT_WIDTH), lambda i, j: (i, layer, j, 0))
        for arr in kv_all:
            aliases[len(args)] = len(out_specs)
            in_specs.append(pl.BlockSpec(memory_space=pl.ANY))
            out_specs.append(layer_spec)
            out_shape.append(jax.ShapeDtypeStruct(arr.shape, arr.dtype))
            args.append(arr)
    return pl.pallas_call(
        _in_proj_kernel if kv_all is None else _in_proj_kv_kernel,
        grid=(b, t // tm),
        in_specs=in_specs,
        out_specs=out_specs,
        out_shape=out_shape,
        input_output_aliases=aliases,
        compiler_params=_cparams(("parallel", "parallel"), 48),
        name="in_proj",
    )(*args)


def _head_masks(shape):
    lane = lax.broadcasted_iota(jnp.int32, shape, 1)
    return lane < HEAD_DIM, lane >= HEAD_DIM


def _scores(qh, k):
    return lax.dot_general(qh, k, (((1,), (1,)), ((), ())), preferred_element_type=F32)


def _ctx_attn_kernel(q_ref, k_ref, v_ref, o_ref):
    t = q_ref.shape[2]
    own_lanes = ((lax.broadcasted_iota(jnp.int32, (2 * t, LANES), 0) < t)
                 == (lax.broadcasted_iota(jnp.int32, (2 * t, LANES), 1) < HEAD_DIM))
    lo_lanes, _ = _head_masks((t, LANES))
    scores = []
    for hp in range(HEAD_PAIRS):
        q2 = jnp.concatenate([q_ref[0, hp], q_ref[0, hp]], axis=0)
        scores.append(_scores(jnp.where(own_lanes, q2, jnp.zeros_like(q2)), k_ref[0, hp]))
    for hp, s in enumerate(scores):
        e = jnp.exp(s - jnp.max(s, axis=-1, keepdims=True))
        l = jnp.sum(e, axis=-1, keepdims=True)
        o2 = jnp.dot(e.astype(BF16), v_ref[0, hp], preferred_element_type=F32) / l
        o_ref[0, :, hp * LANES:(hp + 1) * LANES] = jnp.where(lo_lanes, o2[:t], o2[t:]).astype(BF16)


def _ctx_attention(q, k, v):
    b, _, t, _ = q.shape
    spec = pl.BlockSpec((1, HEAD_PAIRS, t, LANES), lambda i: (i, 0, 0, 0))
    return pl.pallas_call(
        _ctx_attn_kernel,
        grid=(b,),
        in_specs=[spec] * 3,
        out_specs=pl.BlockSpec((1, t, ATT_WIDTH), lambda i: (i, 0, 0)),
        out_shape=jax.ShapeDtypeStruct((b, t, ATT_WIDTH), BF16),
        compiler_params=_cparams(("parallel",), 32),
        name="ctx_attention",
    )(q, k, v)


def _lat_attn_kernel(q_ref, k_ref, v_ref, kc_ref, vc_ref, t2_ref, o_ref, *, rows_per_step, n_rows):
    rb = pl.program_id(2)
    kc, vc = kc_ref[0, 0], vc_ref[0, 0]
    n_loc = WIN_ROWS * GRID_W
    shape2 = (2 * GRID_W, LANES)
    own_lanes = ((lax.broadcasted_iota(jnp.int32, shape2, 0) < GRID_W)
                 == (lax.broadcasted_iota(jnp.int32, shape2, 1) < HEAD_DIM))
    lo_lanes, _ = _head_masks((GRID_W, LANES))

    def scores(i):
        r = rb * rows_per_step + i
        rs = jnp.clip(r - WIN_ROWS // 2, 0, n_rows - WIN_ROWS)
        off = rs - r + WIN_ROWS - 1
        q = q_ref[0, 0, i * GRID_W:(i + 1) * GRID_W, :]
        q2 = jnp.concatenate([q, q], axis=0)
        q2 = jnp.where(own_lanes, q2, jnp.zeros_like(q2))
        k_start = pl.multiple_of(rs * GRID_W, GRID_W)
        bias = jnp.concatenate([t2_ref[0, off + 2 * j] for j in range(WIN_ROWS // 2)], axis=1)
        s_loc = _scores(q2, k_ref[0, 0, pl.ds(k_start, n_loc), :]) + bias
        s_ctx = _scores(q2, kc)
        return jnp.concatenate([s_loc, s_ctx], axis=1), k_start

    def finish(i, s, m, k_start):
        e = jnp.exp(s - m)
        l = jnp.sum(e, axis=-1, keepdims=True)
        e = e.astype(BF16)
        o2 = (jnp.dot(e[:, :n_loc], v_ref[0, 0, pl.ds(k_start, n_loc), :], preferred_element_type=F32)
              + jnp.dot(e[:, n_loc:], vc, preferred_element_type=F32)) / l
        o_ref[0, i * GRID_W:(i + 1) * GRID_W, :] = jnp.where(lo_lanes, o2[:GRID_W], o2[GRID_W:]).astype(BF16)

    rows = [scores(0), scores(1)] + [None] * rows_per_step
    maxima = [jnp.max(rows[0][0], axis=-1, keepdims=True)] + [None] * rows_per_step
    for i in range(rows_per_step):
        if i + 2 < rows_per_step:
            rows[i + 2] = scores(i + 2)
        if i + 1 < rows_per_step:
            maxima[i + 1] = jnp.max(rows[i + 1][0], axis=-1, keepdims=True)
        finish(i, rows[i][0], maxima[i], rows[i][1])
        rows[i] = maxima[i] = None


def _lat_attention(q, k, v, kc, vc, t2):
    b, _, t, _ = q.shape
    lc = kc.shape[2]
    n_rows = t // GRID_W
    rows_per_step = min(64, n_rows)
    tq = rows_per_step * GRID_W
    full = pl.BlockSpec((1, 1, t, LANES), lambda i, j, r: (i, j, 0, 0))
    ctx = pl.BlockSpec((1, 1, lc, LANES), lambda i, j, r: (i, j, 0, 0))
    return pl.pallas_call(
        functools.partial(_lat_attn_kernel, rows_per_step=rows_per_step, n_rows=n_rows),
        grid=(b, HEAD_PAIRS, n_rows // rows_per_step),
        in_specs=[
            pl.BlockSpec((1, 1, tq, LANES), lambda i, j, r: (i, j, r, 0)),
            full, full, ctx, ctx,
            pl.BlockSpec((1, N_BIAS_TILES, LANES, LANES), lambda i, j, r: (j, 0, 0, 0)),
        ],
        out_specs=pl.BlockSpec((1, tq, LANES), lambda i, j, r: (i, r, j)),
        out_shape=jax.ShapeDtypeStruct((b, t, ATT_WIDTH), BF16),
        compiler_params=_cparams(("parallel", "parallel", "arbitrary"), 32),
        name="lat_attention",
    )(q, k, v, kc, vc, t2)


def _log_sigmoid(x):
    return jnp.minimum(x, 0.0) - jnp.log1p(jnp.exp(-jnp.abs(x)))


def _gate_matmuls(xh, wg_ref):
    half = LRU_WIDTH // 2
    xhb = xh.astype(BF16)
    return [jnp.dot(xhb[:, hf * half:(hf + 1) * half], wg_ref[hf], preferred_element_type=F32) for hf in range(2)]


def _lru_coefficients(pre, xh, bg_ref, neg_log_sig4, neg_log2_sig4):
    half = LRU_WIDTH // 2
    t_r = jnp.tanh(jnp.concatenate([pre[0][:, :half], pre[1][:, :half]], axis=1) + bg_ref[0:1, :])
    t_i = jnp.tanh(jnp.concatenate([pre[0][:, half:], pre[1][:, half:]], axis=1) + bg_ref[1:2, :])
    gate = t_r + 1.0
    a = jnp.exp2(gate * neg_log2_sig4)
    z = jnp.tanh(gate * neg_log_sig4) * (1.0 + a * a)
    root = z * lax.rsqrt(jnp.maximum(z, F32_TINY))
    return a, (root * (t_i + 1.0)) * xh


def _pipelined_recurrence(blocks, gate_inputs, xh_block, bg_ref, scales, h, recur):
    m = LRU_TIME_BLOCK * LRU_GROUP
    pre = gate_inputs(blocks[0])
    for pos, blk in enumerate(blocks):
        following = gate_inputs(blocks[pos + 1]) if pos + 1 < len(blocks) else None
        a, b = _lru_coefficients(pre, xh_block(slice(blk * m, (blk + 1) * m)), bg_ref, *scales)
        h = recur(blk, a, b, h)
        pre = following
    return h


def _log_sigmoid_scales(la_ref):
    neg_log_sig4 = (-0.5 * LRU_C) * _log_sigmoid(la_ref[...])
    return neg_log_sig4, neg_log_sig4 * (-LOG2_E)


def _lru_bwd_kernel(xb_ref, xprev_ref, xnext_ref, cw_ref, cb_ref, wg_ref, bg_ref, la_ref, h0_ref,
                    xh_ref, hb_ref, hfin_ref, x_s, carry_s, *, tc):
    j = pl.program_id(1)
    n = pl.num_programs(1)
    cj = n - 1 - j
    x_pitch = tc + X_PAD
    n_slabs = LRU_WIDTH // LANES
    lane_slab = lambda v, c: v[:, c * LANES:(c + 1) * LANES]

    @pl.when(j == 0)
    def _():
        carry_s[...] = h0_ref[0]

    def park(bi, carry):
        base = pl.multiple_of(bi * x_pitch, SUBLANES)
        pieces = ((0, jnp.where(cj > 0, xprev_ref[bi], 0.0)), (SUBLANES, xb_ref[bi]),
                  (SUBLANES + tc, jnp.where(cj < n - 1, xnext_ref[bi], 0.0)))
        for start, v in pieces:
            for c in range(n_slabs):
                x_s[c, pl.ds(base + start, v.shape[0]), :] = lane_slab(v, c)
        return carry

    lax.fori_loop(0, LRU_GROUP, park, 0)

    scales = _log_sigmoid_scales(la_ref)

    def time_rows(t):
        rows = pl.ds(SUBLANES + t, LRU_GROUP, stride=x_pitch)
        return jnp.concatenate([x_s[c, rows, :] for c in range(n_slabs)], axis=1)

    m = LRU_TIME_BLOCK * LRU_GROUP

    def conv_block(blk):
        t0 = blk * LRU_TIME_BLOCK
        xw = jnp.concatenate([time_rows(t0 + k) for k in range(-CONV_LEFT, LRU_TIME_BLOCK + CONV_W - 1 - CONV_LEFT)],
                             axis=0)
        xh = cb_ref[...]
        for tap in range(CONV_W):
            xh = xh + cw_ref[tap:tap + 1, :] * xw[tap * LRU_GROUP:tap * LRU_GROUP + m]
        xh_ref[0, blk * m:(blk + 1) * m, :] = xh
        return _gate_matmuls(xh, wg_ref)

    def recur(blk, a, b, h):
        for i in reversed(range(LRU_TIME_BLOCK)):
            rows = slice(i * LRU_GROUP, (i + 1) * LRU_GROUP)
            h = a[rows] * h + b[rows]
            hb_ref[0, blk * m + i * LRU_GROUP:blk * m + (i + 1) * LRU_GROUP, :] = h
        return h

    blocks = list(reversed(range(tc // LRU_TIME_BLOCK)))
    carry_s[...] = _pipelined_recurrence(blocks, conv_block, lambda rows: xh_ref[0, rows, :], bg_ref, scales,
                                         carry_s[...], recur)

    @pl.when(j == n - 1)
    def _():
        hfin_ref[0] = carry_s[...]


def _lru_fwd_kernel(xh_ref, hb_ref, gy_ref, wg_ref, bg_ref, la_ref, h0_ref, out_ref, hfin_ref,
                    s_s, carry_s, *, tc):
    j = pl.program_id(1)
    n = pl.num_programs(1)
    pitch = tc + SCAN_PAD
    n_slabs = LRU_WIDTH // LANES
    m = LRU_TIME_BLOCK * LRU_GROUP

    @pl.when(j == 0)
    def _():
        carry_s[...] = h0_ref[0]

    scales = _log_sigmoid_scales(la_ref)

    def recur(blk, a, b, h):
        for i in range(LRU_TIME_BLOCK):
            rows = slice(i * LRU_GROUP, (i + 1) * LRU_GROUP)
            h = a[rows] * h + b[rows]
            both = h + hb_ref[0, blk * m + i * LRU_GROUP:blk * m + (i + 1) * LRU_GROUP, :]
            for c in range(n_slabs):
                s_s[c, pl.ds(blk * LRU_TIME_BLOCK + i, LRU_GROUP, stride=pitch), :] = both[:, c * LANES:(c + 1) * LANES]
        return h

    xh_block = lambda rows: xh_ref[0, rows, :]
    carry_s[...] = _pipelined_recurrence(
        list(range(tc // LRU_TIME_BLOCK)),
        lambda blk: _gate_matmuls(xh_block(slice(blk * m, (blk + 1) * m)), wg_ref),
        xh_block, bg_ref, scales, carry_s[...], recur)

    def emit(bi, carry):
        rows = pl.ds(pl.multiple_of(bi * pitch, SUBLANES), tc)
        both = jnp.concatenate([s_s[c, rows, :] for c in range(n_slabs)], axis=1)
        out_ref[bi] = (both * gy_ref[bi]).astype(BF16)
        return carry

    lax.fori_loop(0, LRU_GROUP, emit, 0)

    @pl.when(j == n - 1)
    def _():
        hfin_ref[0] = carry_s[...]


def _lru(xb, gy, conv_w, conv_b, wg, bg, la, h0):
    b, t, c = xb.shape
    assert b % LRU_GROUP == 0
    groups = b // LRU_GROUP
    tc = min(256, t)
    n = t // tc
    per = tc // SUBLANES
    assert tc % (2 * SUBLANES) == 0 and tc % LRU_TIME_BLOCK == 0
    h0 = h0.reshape(groups, LRU_GROUP, 2, c)
    state = pl.BlockSpec((1, LRU_GROUP, c), lambda g, j: (g, 0, 0))
    state_shape = jax.ShapeDtypeStruct((groups, LRU_GROUP, c), F32)
    tm_rows = tc * LRU_GROUP
    tm_shape = jax.ShapeDtypeStruct((groups, t * LRU_GROUP, c), F32)
    slabs = lambda pad: pltpu.VMEM((c // LANES, LRU_GROUP * (tc + pad), LANES), F32)
    carry = pltpu.VMEM((LRU_GROUP, c), F32)
    gate_specs = [_resident((2, c // 2, c)), _resident((2, c)), _resident((1, c)), state]

    rev = lambda j: n - 1 - j
    rev_tm = pl.BlockSpec((1, tm_rows, c), lambda g, j: (g, rev(j), 0))
    halo = lambda f: pl.BlockSpec((LRU_GROUP, SUBLANES, c), lambda g, j: (g, f(rev(j)), 0))
    xh, hb, hb_fin = pl.pallas_call(
        functools.partial(_lru_bwd_kernel, tc=tc),
        grid=(groups, n),
        in_specs=[
            pl.BlockSpec((LRU_GROUP, tc, c), lambda g, j: (g, rev(j), 0)),
            halo(lambda cj: jnp.maximum(cj * per - 1, 0)),
            halo(lambda cj: jnp.minimum((cj + 1) * per, t // SUBLANES - 1)),
            _resident((CONV_W, c)),
            _resident((1, c)),
        ] + gate_specs,
        out_specs=[rev_tm, rev_tm, state],
        out_shape=[tm_shape, tm_shape, state_shape],
        scratch_shapes=[slabs(X_PAD), carry],
        compiler_params=_cparams(("parallel", "arbitrary"), 56),
        name="lru_bwd",
    )(xb, xb, xb, conv_w, conv_b, wg[1], bg[1], la[1:2], h0[:, :, 1])

    fwd_tm = pl.BlockSpec((1, tm_rows, c), lambda g, j: (g, j, 0))
    chunk = pl.BlockSpec((LRU_GROUP, tc, c), lambda g, j: (g, j, 0))
    rec, hf_fin = pl.pallas_call(
        functools.partial(_lru_fwd_kernel, tc=tc),
        grid=(groups, n),
        in_specs=[fwd_tm, fwd_tm, chunk] + gate_specs,
        out_specs=[chunk, state],
        out_shape=[jax.ShapeDtypeStruct((b, t, c), BF16), state_shape],
        scratch_shapes=[slabs(SCAN_PAD), carry],
        compiler_params=_cparams(("parallel", "arbitrary"), 56),
        name="lru_fwd",
    )(xh, hb, gy, wg[0], bg[0], la[0:1], h0[:, :, 0])
    return rec, jnp.stack([hf_fin.reshape(b, c), hb_fin.reshape(b, c)], axis=1)


def _out_ffn_kernel(x_ref, att_ref, rec_ref, mod_ref, g_ref, wo_ref, wg_ref, wu_ref, wd_ref, *rest, final_norm):
    o_ref = rest[-1]
    mix = jnp.concatenate([att_ref[0], rec_ref[0]], axis=1)
    x1 = x_ref[0] + mod_ref[0, 2:3, :] * jnp.dot(mix, wo_ref[...], preferred_element_type=F32)
    h2 = (_rmsnorm(x1, g_ref[...]) * (1.0 + mod_ref[0, 4:5, :]) + mod_ref[0, 3:4, :]).astype(BF16)
    gate = jnp.dot(h2, wg_ref[...], preferred_element_type=F32)
    up = jnp.dot(h2, wu_ref[...], preferred_element_type=F32)
    act = ((gate * jax.nn.sigmoid(gate)) * up).astype(BF16)
    x2 = x1 + mod_ref[0, 5:6, :] * jnp.dot(act, wd_ref[...], preferred_element_type=F32)
    if final_norm:
        x2 = _rmsnorm(x2, rest[0][...])
    o_ref[0] = x2


def _out_ffn(x, att, rec, mod, g2, w_out, w_gate, w_up, w_down, layer, g_final=None):
    b, t, d = x.shape
    d_ff = w_gate.shape[-1]
    tm = min(512, t)
    shared = mod.shape[0] == 1
    mod_map = (lambda i, j: (0, 0, 0)) if shared else (lambda i, j: (i, 0, 0))
    tok = lambda w: pl.BlockSpec((1, tm, w), lambda i, j: (i, j, 0))
    in_specs = [
        tok(d), tok(ATT_WIDTH), tok(LRU_WIDTH),
        pl.BlockSpec((1, 6, d), mod_map),
        _resident((1, d)),
        _layer_resident((d, d), layer),
        _layer_resident((d, d_ff), layer),
        _layer_resident((d, d_ff), layer),
        _layer_resident((d_ff, d), layer),
    ]
    args = [x, att, rec, mod, g2, w_out, w_gate, w_up, w_down]
    if g_final is not None:
        in_specs.append(_resident((1, d)))
        args.append(g_final)
    return pl.pallas_call(
        functools.partial(_out_ffn_kernel, final_norm=g_final is not None),
        grid=(b, t // tm),
        in_specs=in_specs,
        out_specs=tok(d),
        out_shape=jax.ShapeDtypeStruct((b, t, d), F32),
        compiler_params=_cparams(("parallel", "parallel"), 56),
        name="out_ffn",
    )(*args)


def _block_diag_halves(w):
    per_half = LRU_BLOCKS // 2
    w4 = w.reshape(2, per_half, LRU_BLOCK, LRU_BLOCK)
    eye = jnp.eye(per_half, dtype=w.dtype)
    return jnp.einsum('hnjk,nm->hnjmk', w4, eye).reshape(2, per_half * LRU_BLOCK, per_half * LRU_BLOCK)


def _head_pair_layout(cache):
    b, depth, p = cache.shape[:3]
    return cache.reshape(b, depth, p, HEAD_PAIRS, LANES).transpose(1, 0, 3, 2, 4).astype(BF16)


def kernel(x_prompt, x_sample, cache_k, cache_v, state_lru, c, c_ctx, w_mod, b_mod, norm1, norm2, w_in, w_out, rpb, conv_w, conv_b, lru_a, lru_wr, lru_br, lru_wi, lru_bi, w_gate, w_up, w_down, norm_final):
    depth = w_in.shape[0]
    n_ctx, n_lat = x_prompt.shape[0], x_sample.shape[0]
    d = x_prompt.shape[-1]
    assert n_lat + 1 <= MOD_ROWS

    cvec = jnp.zeros((MOD_ROWS, d), F32).at[:n_lat].set(c).at[n_lat].set(c_ctx)
    mod = _modulation(cvec, w_mod, b_mod).reshape(depth, MOD_ROWS, 6, d)
    t2 = _bias_tables(rpb)
    ck, cv = _head_pair_layout(cache_k), _head_pair_layout(cache_v)

    w_in_b, w_out_b = w_in.astype(BF16), w_out.astype(BF16)
    w_gate_b, w_up_b, w_down_b = w_gate.astype(BF16), w_up.astype(BF16), w_down.astype(BF16)
    wg = jnp.stack([
        jnp.stack([jnp.concatenate([_block_diag_halves(lru_wr[l, dr]), _block_diag_halves(lru_wi[l, dr])], axis=-1)
                   for dr in range(2)]) for l in range(depth)]).astype(BF16)
    bg = 0.5 * jnp.stack([lru_br, lru_bi], axis=2)
    conv_w_half, conv_b_half = 0.5 * conv_w, 0.5 * conv_b
    h0_ctx = jnp.zeros((n_ctx, 2, LRU_WIDTH), F32)

    def layer(x, l, mod_l, attend, h0, last, kv_all=None):
        outs = _in_proj(x, mod_l, norm1[l][None], w_in_b, l, kv_all)
        q, k, v, xb, gy = outs[:5]
        att = attend(q, k, v)
        rec, h_fin = _lru(xb, gy, conv_w_half[l], conv_b_half[l][None], wg[l], bg[l], lru_a[l], h0)
        x = _out_ffn(x, att, rec, mod_l, norm2[l][None], w_out_b, w_gate_b, w_up_b, w_down_b, l,
                     g_final=norm_final[None] if last else None)
        return x, outs[5:], h_fin

    xp, xs = x_prompt, x_sample
    kv_all = [jnp.zeros((n_ctx, depth, x_prompt.shape[1], ATT_WIDTH), F32) for _ in range(2)]
    hs_out = []
    for l in range(depth):
        last = l == depth - 1
        xp, kv_all, h_l = layer(xp, l, mod[l, n_lat:n_lat + 1], _ctx_attention, h0_ctx, last, kv_all)
        hs_out.append(h_l)
        attend_lat = functools.partial(_lat_attention, kc=ck[l], vc=cv[l], t2=t2[l])
        xs, _, _ = layer(xs, l, mod[l, :n_lat], attend_lat, state_lru[:, l], last)
    new_k, new_v = (a.reshape(n_ctx, depth, -1, N_ATT_HEADS, HEAD_DIM) for a in kv_all)
    return (xp, xs, new_k, new_v, jnp.stack(hs_out, axis=1))
```

```python
import functools

import jax
import jax.numpy as jnp
from jax import lax
from jax.experimental import pallas as pl
from jax.experimental.pallas import tpu as pltpu

F32 = jnp.float32
BF16 = jnp.bfloat16

D_MODEL = 1024
GRID_W = 64
N_ATT_HEADS = 8
HEAD_DIM = 64
ATT_WIDTH = N_ATT_HEADS * HEAD_DIM
ATT_SCALE = HEAD_DIM ** -0.5
WIN_ROWS = 8
WIN_COLS = 16
LRU_WIDTH = D_MODEL // 2
LRU_BLOCKS = 8
LRU_BLOCK = LRU_WIDTH // LRU_BLOCKS
CONV_W = 4
CONV_LEFT = 2
LRU_C = 8.0
IN_COLS = 3 * ATT_WIDTH + 2 * LRU_WIDTH
EPS = 1e-6
NEG_INF = -1e30
F32_TINY = float(jnp.finfo(jnp.float32).tiny)
LOG2_E = 1.4426950408889634

LANES = 128
SUBLANES = 8
HEAD_PAIRS = ATT_WIDTH // LANES
N_ROW_IDX = 2 * WIN_ROWS - 1
N_COL_IDX = 2 * WIN_COLS - 1
N_BIAS_TILES = N_ROW_IDX - 1
MOD_ROWS = 16
LRU_GROUP = SUBLANES
SCAN_PAD = SUBLANES
X_PAD = 3 * SUBLANES
LRU_TIME_BLOCK = 32
SIDE_FFN_CHUNK = 128
MIB = 1024 * 1024


def _cparams(semantics, vmem_mib):
    return pltpu.CompilerParams(dimension_semantics=semantics, vmem_limit_bytes=vmem_mib * MIB)


def _resident(shape):
    zeros = (0,) * len(shape)
    return pl.BlockSpec(shape, lambda *_: zeros, pipeline_mode=pl.Buffered(1))


def _layer_resident(shape, layer):
    index = (layer,) + (0,) * len(shape)
    return pl.BlockSpec((None,) + tuple(shape), lambda *_: index, pipeline_mode=pl.Buffered(1))


def _rmsnorm(x, g):
    return (x * lax.rsqrt(jnp.mean(x * x, axis=-1, keepdims=True) + EPS)) * g


def _mod_kernel(c_ref, w_ref, b_ref, o_ref):
    cv = c_ref[...]
    s = (cv * jax.nn.sigmoid(cv)).astype(BF16)
    o_ref[0] = jnp.dot(s, w_ref[0].astype(BF16), preferred_element_type=F32) + b_ref[0]


def _modulation(cvec, w_mod, b_mod):
    depth, d, n = w_mod.shape
    tn = n // 4
    return pl.pallas_call(
        _mod_kernel,
        grid=(depth, n // tn),
        in_specs=[
            pl.BlockSpec((MOD_ROWS, d), lambda l, j: (0, 0)),
            pl.BlockSpec((1, d, tn), lambda l, j: (l, 0, j)),
            pl.BlockSpec((1, 1, tn), lambda l, j: (l, 0, j)),
        ],
        out_specs=pl.BlockSpec((1, MOD_ROWS, tn), lambda l, j: (l, 0, j)),
        out_shape=jax.ShapeDtypeStruct((depth, MOD_ROWS, n), F32),
        compiler_params=_cparams(("arbitrary", "arbitrary"), 40),
        name="adaln_mod",
    )(cvec, w_mod, b_mod.reshape(depth, 1, n))


def _bias_table_kernel(rpb_ref, o_ref):
    shape = (GRID_W, LANES)
    c = lax.broadcasted_iota(jnp.int32, shape, 0)
    lane = lax.broadcasted_iota(jnp.int32, shape, 1)
    upper = lane >= GRID_W
    kc = jnp.where(upper, lane - GRID_W, lane)
    cs = jnp.clip(c - WIN_COLS // 2, 0, GRID_W - WIN_COLS)
    ok = (kc >= cs) & (kc < cs + WIN_COLS)
    lo_shift = LANES - (WIN_COLS - 1)
    hi_shift = (lo_shift + GRID_W) % LANES
    for h in range(2):
        for i in range(N_BIAS_TILES):
            lo = pltpu.roll(jnp.broadcast_to(rpb_ref[0, h, i:i + 1, :], shape), lo_shift, 1, stride=1, stride_axis=0)
            hi = pltpu.roll(jnp.broadcast_to(rpb_ref[0, h, i + 1:i + 2, :], shape), hi_shift, 1, stride=1, stride_axis=0)
            o_ref[0, i, h * GRID_W:(h + 1) * GRID_W, :] = jnp.where(ok, jnp.where(upper, hi, lo), NEG_INF)


def _bias_tables(rpb):
    depth = rpb.shape[0]
    n = depth * HEAD_PAIRS
    padded = jnp.pad(rpb, ((0, 0), (0, 0), (0, 2 * SUBLANES - N_ROW_IDX), (0, LANES - N_COL_IDX)))
    out = pl.pallas_call(
        _bias_table_kernel,
        grid=(n,),
        in_specs=[pl.BlockSpec((1, 2, 2 * SUBLANES, LANES), lambda i: (i, 0, 0, 0))],
        out_specs=pl.BlockSpec((1, N_BIAS_TILES, LANES, LANES), lambda i: (i, 0, 0, 0)),
        out_shape=jax.ShapeDtypeStruct((n, N_BIAS_TILES, LANES, LANES), F32),
        compiler_params=_cparams(("arbitrary",), 16),
        name="bias_tables",
    )(padded.reshape(n, 2, 2 * SUBLANES, LANES))
    return out.reshape(depth, HEAD_PAIRS, N_BIAS_TILES, LANES, LANES)


def _in_proj_kernel(x_ref, mod_ref, g_ref, w_ref, q_ref, k_ref, v_ref, xb_ref, gy_ref, *kv_f32_refs):
    h = _rmsnorm(x_ref[0], g_ref[...]) * (1.0 + mod_ref[0, 1:2, :]) + mod_ref[0, 0:1, :]
    p = jnp.dot(h.astype(BF16), w_ref[...], preferred_element_type=F32)
    for j in range(HEAD_PAIRS):
        lo = j * LANES
        q_ref[0, j] = (p[:, lo:lo + LANES] * ATT_SCALE).astype(BF16)
        k_ref[0, j] = p[:, ATT_WIDTH + lo:ATT_WIDTH + lo + LANES].astype(BF16)
        v_ref[0, j] = p[:, 2 * ATT_WIDTH + lo:2 * ATT_WIDTH + lo + LANES].astype(BF16)
    xb_ref[0] = p[:, 3 * ATT_WIDTH:3 * ATT_WIDTH + LRU_WIDTH]
    gy_ref[0] = jax.nn.gelu(p[:, 3 * ATT_WIDTH + LRU_WIDTH:])
    if kv_f32_refs:
        kf_ref, vf_ref = kv_f32_refs
        kf_ref[0] = p[:, ATT_WIDTH:2 * ATT_WIDTH]
        vf_ref[0] = p[:, 2 * ATT_WIDTH:3 * ATT_WIDTH]


def _in_proj_kv_kernel(x_ref, mod_ref, g_ref, w_ref, k_all_ref, v_all_ref, *out_refs):
    del k_all_ref, v_all_ref
    _in_proj_kernel(x_ref, mod_ref, g_ref, w_ref, *out_refs)


def _in_proj(x, mod, g, w, layer, kv_all=None):
    b, t, d = x.shape
    tm = min(1024, t)
    shared = mod.shape[0] == 1
    mod_map = (lambda i, j: (0, 0, 0)) if shared else (lambda i, j: (i, 0, 0))
    heads_spec = pl.BlockSpec((1, HEAD_PAIRS, tm, LANES), lambda i, j: (i, 0, j, 0))
    heads_shape = jax.ShapeDtypeStruct((b, HEAD_PAIRS, t, LANES), BF16)
    half_spec = pl.BlockSpec((1, tm, LRU_WIDTH), lambda i, j: (i, j, 0))
    half_shape = jax.ShapeDtypeStruct((b, t, LRU_WIDTH), F32)
    in_specs = [
        pl.BlockSpec((1, tm, d), lambda i, j: (i, j, 0)),
        pl.BlockSpec((1, 6, d), mod_map),
        _resident((1, d)),
        _layer_resident((d, IN_COLS), layer),
    ]
    out_specs = [heads_spec] * 3 + [half_spec] * 2
    out_shape = [heads_shape] * 3 + [half_shape] * 2
    args = [x, mod, g, w]
    aliases = {}
    if kv_all is not None:
        layer_spec = pl.BlockSpec((1, None, tm, ATT_WIDTH), lambda i, j: (i, layer, j, 0))
        for arr in kv_all:
            aliases[len(args)] = len(out_specs)
            in_specs.append(pl.BlockSpec(memory_space=pl.ANY))
            out_specs.append(layer_spec)
            out_shape.append(jax.ShapeDtypeStruct(arr.shape, arr.dtype))
            args.append(arr)
    return pl.pallas_call(
        _in_proj_kernel if kv_all is None else _in_proj_kv_kernel,
        grid=(b, t // tm),
        in_specs=in_specs,
        out_specs=out_specs,
        out_shape=out_shape,
        input_output_aliases=aliases,
        compiler_params=_cparams(("parallel", "parallel"), 48),
        name="in_proj",
    )(*args)


def _head_masks(shape):
    lane = lax.broadcasted_iota(jnp.int32, shape, 1)
    return lane < HEAD_DIM, lane >= HEAD_DIM


def _scores(qh, k):
    return lax.dot_general(qh, k, (((1,), (1,)), ((), ())), preferred_element_type=F32)


def _ctx_attn_kernel(q_ref, k_ref, v_ref, o_ref):
    t = q_ref.shape[2]
    own_lanes = ((lax.broadcasted_iota(jnp.int32, (2 * t, LANES), 0) < t)
                 == (lax.broadcasted_iota(jnp.int32, (2 * t, LANES), 1) < HEAD_DIM))
    lo_lanes, _ = _head_masks((t, LANES))
    scores = []
    for hp in range(HEAD_PAIRS):
        q2 = jnp.concatenate([q_ref[0, hp], q_ref[0, hp]], axis=0)
        scores.append(_scores(jnp.where(own_lanes, q2, jnp.zeros_like(q2)), k_ref[0, hp]))
    for hp, s in enumerate(scores):
        e = jnp.exp(s - jnp.max(s, axis=-1, keepdims=True))
        l = jnp.sum(e, axis=-1, keepdims=True)
        o2 = jnp.dot(e.astype(BF16), v_ref[0, hp], preferred_element_type=F32) / l
        o_ref[0, :, hp * LANES:(hp + 1) * LANES] = jnp.where(lo_lanes, o2[:t], o2[t:]).astype(BF16)


def _ctx_attention(q, k, v):
    b, _, t, _ = q.shape
    spec = pl.BlockSpec((1, HEAD_PAIRS, t, LANES), lambda i: (i, 0, 0, 0))
    return pl.pallas_call(
        _ctx_attn_kernel,
        grid=(b,),
        in_specs=[spec] * 3,
        out_specs=pl.BlockSpec((1, t, ATT_WIDTH), lambda i: (i, 0, 0)),
        out_shape=jax.ShapeDtypeStruct((b, t, ATT_WIDTH), BF16),
        compiler_params=_cparams(("parallel",), 32),
        name="ctx_attention",
    )(q, k, v)


def _lat_attn_kernel(q_ref, k_ref, v_ref, kc_ref, vc_ref, t2_ref, o_ref, *, rows_per_step, n_rows):
    rb = pl.program_id(2)
    kc, vc = kc_ref[0, 0], vc_ref[0, 0]
    n_loc = WIN_ROWS * GRID_W
    shape2 = (2 * GRID_W, LANES)
    own_lanes = ((lax.broadcasted_iota(jnp.int32, shape2, 0) < GRID_W)
                 == (lax.broadcasted_iota(jnp.int32, shape2, 1) < HEAD_DIM))
    lo_lanes, _ = _head_masks((GRID_W, LANES))

    def scores(i):
        r = rb * rows_per_step + i
        rs = jnp.clip(r - WIN_ROWS // 2, 0, n_rows - WIN_ROWS)
        off = rs - r + WIN_ROWS - 1
        q = q_ref[0, 0, i * GRID_W:(i + 1) * GRID_W, :]
        q2 = jnp.concatenate([q, q], axis=0)
        q2 = jnp.where(own_lanes, q2, jnp.zeros_like(q2))
        k_start = pl.multiple_of(rs * GRID_W, GRID_W)
        bias = jnp.concatenate([t2_ref[0, off + 2 * j] for j in range(WIN_ROWS // 2)], axis=1)
        s_loc = _scores(q2, k_ref[0, 0, pl.ds(k_start, n_loc), :]) + bias
        s_ctx = _scores(q2, kc)
        return jnp.concatenate([s_loc, s_ctx], axis=1), k_start

    def finish(i, s, m, k_start):
        e = jnp.exp(s - m)
        l = jnp.sum(e, axis=-1, keepdims=True)
        e = e.astype(BF16)
        o2 = (jnp.dot(e[:, :n_loc], v_ref[0, 0, pl.ds(k_start, n_loc), :], preferred_element_type=F32)
              + jnp.dot(e[:, n_loc:], vc, preferred_element_type=F32)) / l
        o_ref[0, i * GRID_W:(i + 1) * GRID_W, :] = jnp.where(lo_lanes, o2[:GRID_W], o2[GRID_W:]).astype(BF16)

    rows = [scores(0), scores(1)] + [None] * rows_per_step
    maxima = [jnp.max(rows[0][0], axis=-1, keepdims=True)] + [None] * rows_per_step
    for i in range(rows_per_step):
        if i + 2 < rows_per_step:
            rows[i + 2] = scores(i + 2)
        if i + 1 < rows_per_step:
            maxima[i + 1] = jnp.max(rows[i + 1][0], axis=-1, keepdims=True)
        finish(i, rows[i][0], maxima[i], rows[i][1])
        rows[i] = maxima[i] = None


def _lat_attention(q, k, v, kc, vc, t2):
    b, _, t, _ = q.shape
    lc = kc.shape[2]
    n_rows = t // GRID_W
    rows_per_step = min(32, n_rows)
    tq = rows_per_step * GRID_W
    full = pl.BlockSpec((1, 1, t, LANES), lambda i, j, r: (i, j, 0, 0))
    ctx = pl.BlockSpec((1, 1, lc, LANES), lambda i, j, r: (i, j, 0, 0))
    return pl.pallas_call(
        functools.partial(_lat_attn_kernel, rows_per_step=rows_per_step, n_rows=n_rows),
        grid=(b, HEAD_PAIRS, n_rows // rows_per_step),
        in_specs=[
            pl.BlockSpec((1, 1, tq, LANES), lambda i, j, r: (i, j, r, 0)),
            full, full, ctx, ctx,
            pl.BlockSpec((1, N_BIAS_TILES, LANES, LANES), lambda i, j, r: (j, 0, 0, 0)),
        ],
        out_specs=pl.BlockSpec((1, tq, LANES), lambda i, j, r: (i, r, j)),
        out_shape=jax.ShapeDtypeStruct((b, t, ATT_WIDTH), BF16),
        compiler_params=_cparams(("parallel", "parallel", "arbitrary"), 32),
        name="lat_attention",
    )(q, k, v, kc, vc, t2)


def _log_sigmoid(x):
    return jnp.minimum(x, 0.0) - jnp.log1p(jnp.exp(-jnp.abs(x)))


def _gate_matmuls(xh, wg_ref):
    half = LRU_WIDTH // 2
    xhb = xh.astype(BF16)
    return [jnp.dot(xhb[:, hf * half:(hf + 1) * half], wg_ref[hf], preferred_element_type=F32) for hf in range(2)]


def _lru_coefficients(pre, xh, bg_ref, neg_log_sig4, neg_log2_sig4):
    half = LRU_WIDTH // 2
    t_r = jnp.tanh(jnp.concatenate([pre[0][:, :half], pre[1][:, :half]], axis=1) + bg_ref[0:1, :])
    t_i = jnp.tanh(jnp.concatenate([pre[0][:, half:], pre[1][:, half:]], axis=1) + bg_ref[1:2, :])
    gate = t_r + 1.0
    a = jnp.exp2(gate * neg_log2_sig4)
    z = jnp.tanh(gate * neg_log_sig4) * (1.0 + a * a)
    root = z * lax.rsqrt(jnp.maximum(z, F32_TINY))
    return a, (root * (t_i + 1.0)) * xh


def _pipelined_recurrence(blocks, gate_inputs, xh_block, bg_ref, scales, h, recur, side_stages=()):
    assert len(side_stages) <= len(blocks)
    m = LRU_TIME_BLOCK * LRU_GROUP
    pre = gate_inputs(blocks[0])
    for pos, blk in enumerate(blocks):
        if pos < len(side_stages):
            side_stages[pos]()
        following = gate_inputs(blocks[pos + 1]) if pos + 1 < len(blocks) else None
        a, b = _lru_coefficients(pre, xh_block(slice(blk * m, (blk + 1) * m)), bg_ref, *scales)
        h = recur(blk, a, b, h)
        pre = following
    return h


def _log_sigmoid_scales(la_ref):
    neg_log_sig4 = (-0.5 * LRU_C) * _log_sigmoid(la_ref[...])
    return neg_log_sig4, neg_log_sig4 * (-LOG2_E)


def _lru_bwd_kernel(xb_ref, xprev_ref, xnext_ref, cw_ref, cb_ref, wg_ref, bg_ref, la_ref, h0_ref,
                    xh_ref, hb_ref, hfin_ref, x_s, carry_s, *, tc):
    j = pl.program_id(1)
    n = pl.num_programs(1)
    cj = n - 1 - j
    x_pitch = tc + X_PAD
    n_slabs = LRU_WIDTH // LANES
    lane_slab = lambda v, c: v[:, c * LANES:(c + 1) * LANES]

    @pl.when(j == 0)
    def _():
        carry_s[...] = h0_ref[0]

    def park(bi, carry):
        base = pl.multiple_of(bi * x_pitch, SUBLANES)
        pieces = ((0, jnp.where(cj > 0, xprev_ref[bi], 0.0)), (SUBLANES, xb_ref[bi]),
                  (SUBLANES + tc, jnp.where(cj < n - 1, xnext_ref[bi], 0.0)))
        for start, v in pieces:
            for c in range(n_slabs):
                x_s[c, pl.ds(base + start, v.shape[0]), :] = lane_slab(v, c)
        return carry

    lax.fori_loop(0, LRU_GROUP, park, 0)

    scales = _log_sigmoid_scales(la_ref)

    def time_rows(t):
        rows = pl.ds(SUBLANES + t, LRU_GROUP, stride=x_pitch)
        return jnp.concatenate([x_s[c, rows, :] for c in range(n_slabs)], axis=1)

    m = LRU_TIME_BLOCK * LRU_GROUP

    def conv_block(blk):
        t0 = blk * LRU_TIME_BLOCK
        xw = jnp.concatenate([time_rows(t0 + k) for k in range(-CONV_LEFT, LRU_TIME_BLOCK + CONV_W - 1 - CONV_LEFT)],
                             axis=0)
        xh = cb_ref[...]
        for tap in range(CONV_W):
            xh = xh + cw_ref[tap:tap + 1, :] * xw[tap * LRU_GROUP:tap * LRU_GROUP + m]
        xh_ref[0, blk * m:(blk + 1) * m, :] = xh
        return _gate_matmuls(xh, wg_ref)

    def recur(blk, a, b, h):
        for i in reversed(range(LRU_TIME_BLOCK)):
            rows = slice(i * LRU_GROUP, (i + 1) * LRU_GROUP)
            h = a[rows] * h + b[rows]
            hb_ref[0, blk * m + i * LRU_GROUP:blk * m + (i + 1) * LRU_GROUP, :] = h
        return h

    blocks = list(reversed(range(tc // LRU_TIME_BLOCK)))
    carry_s[...] = _pipelined_recurrence(blocks, conv_block, lambda rows: xh_ref[0, rows, :], bg_ref, scales,
                                         carry_s[...], recur)

    @pl.when(j == n - 1)
    def _():
        hfin_ref[0] = carry_s[...]


def _lru_fwd_kernel(xh_ref, hb_ref, gy_ref, wg_ref, bg_ref, la_ref, h0_ref, *rest, tc, side_ffn):
    side_stages = ()
    if side_ffn is not None:
        n_in = len(rest) - 5
        side_stages = _ffn_stages(*rest[:n_in], rest[n_in + 2], final_norm=side_ffn)
        rest = rest[n_in:n_in + 2] + rest[n_in + 3:]
    out_ref, hfin_ref, s_s, carry_s = rest
    j = pl.program_id(1)
    n = pl.num_programs(1)
    pitch = tc + SCAN_PAD
    n_slabs = LRU_WIDTH // LANES
    m = LRU_TIME_BLOCK * LRU_GROUP

    @pl.when(j == 0)
    def _():
        carry_s[...] = h0_ref[0]

    scales = _log_sigmoid_scales(la_ref)

    def recur(blk, a, b, h):
        for i in range(LRU_TIME_BLOCK):
            rows = slice(i * LRU_GROUP, (i + 1) * LRU_GROUP)
            h = a[rows] * h + b[rows]
            both = h + hb_ref[0, blk * m + i * LRU_GROUP:blk * m + (i + 1) * LRU_GROUP, :]
            for c in range(n_slabs):
                s_s[c, pl.ds(blk * LRU_TIME_BLOCK + i, LRU_GROUP, stride=pitch), :] = both[:, c * LANES:(c + 1) * LANES]
        return h

    xh_block = lambda rows: xh_ref[0, rows, :]
    carry_s[...] = _pipelined_recurrence(
        list(range(tc // LRU_TIME_BLOCK)),
        lambda blk: _gate_matmuls(xh_block(slice(blk * m, (blk + 1) * m)), wg_ref),
        xh_block, bg_ref, scales, carry_s[...], recur, side_stages)

    def emit(bi, carry):
        rows = pl.ds(pl.multiple_of(bi * pitch, SUBLANES), tc)
        both = jnp.concatenate([s_s[c, rows, :] for c in range(n_slabs)], axis=1)
        out_ref[bi] = (both * gy_ref[bi]).astype(BF16)
        return carry

    lax.fori_loop(0, LRU_GROUP, emit, 0)

    @pl.when(j == n - 1)
    def _():
        hfin_ref[0] = carry_s[...]


def _lru(xb, gy, conv_w, conv_b, wg, bg, la, h0, side_ffn=None):
    b, t, c = xb.shape
    assert b % LRU_GROUP == 0
    groups = b // LRU_GROUP
    tc = min(256, t)
    n = t // tc
    per = tc // SUBLANES
    assert tc % (2 * SUBLANES) == 0 and tc % LRU_TIME_BLOCK == 0
    h0 = h0.reshape(groups, LRU_GROUP, 2, c)
    state = pl.BlockSpec((1, LRU_GROUP, c), lambda g, j: (g, 0, 0))
    state_shape = jax.ShapeDtypeStruct((groups, LRU_GROUP, c), F32)
    tm_rows = tc * LRU_GROUP
    tm_shape = jax.ShapeDtypeStruct((groups, t * LRU_GROUP, c), F32)
    slabs = lambda pad: pltpu.VMEM((c // LANES, LRU_GROUP * (tc + pad), LANES), F32)
    carry = pltpu.VMEM((LRU_GROUP, c), F32)
    gate_specs = [_resident((2, c // 2, c)), _resident((2, c)), _resident((1, c)), state]

    rev = lambda j: n - 1 - j
    rev_tm = pl.BlockSpec((1, tm_rows, c), lambda g, j: (g, rev(j), 0))
    halo = lambda f: pl.BlockSpec((LRU_GROUP, SUBLANES, c), lambda g, j: (g, f(rev(j)), 0))
    xh, hb, hb_fin = pl.pallas_call(
        functools.partial(_lru_bwd_kernel, tc=tc),
        grid=(groups, n),
        in_specs=[
            pl.BlockSpec((LRU_GROUP, tc, c), lambda g, j: (g, rev(j), 0)),
            halo(lambda cj: jnp.maximum(cj * per - 1, 0)),
            halo(lambda cj: jnp.minimum((cj + 1) * per, t // SUBLANES - 1)),
            _resident((CONV_W, c)),
            _resident((1, c)),
        ] + gate_specs,
        out_specs=[rev_tm, rev_tm, state],
        out_shape=[tm_shape, tm_shape, state_shape],
        scratch_shapes=[slabs(X_PAD), carry],
        compiler_params=_cparams(("parallel", "arbitrary"), 56),
        name="lru_bwd",
    )(xb, xb, xb, conv_w, conv_b, wg[1], bg[1], la[1:2], h0[:, :, 1])

    if side_ffn is not None:
        tc = min(SIDE_FFN_CHUNK, tc)
        n, tm_rows = t // tc, tc * LRU_GROUP
    fwd_tm = pl.BlockSpec((1, tm_rows, c), lambda g, j: (g, j, 0))
    chunk = pl.BlockSpec((LRU_GROUP, tc, c), lambda g, j: (g, j, 0))
    in_specs = [fwd_tm, fwd_tm, chunk] + gate_specs
    args = [xh, hb, gy, wg[0], bg[0], la[0:1], h0[:, :, 0]]
    out_specs = [chunk, state]
    out_shape = [jax.ShapeDtypeStruct((b, t, c), BF16), state_shape]
    final_norm = None
    if side_ffn is not None:
        x_side, att_side, rec_side = (a.reshape(1, -1, a.shape[-1]) for a in side_ffn[:3])
        n_side = x_side.shape[1]
        tile = n_side // (groups * n)
        assert tile * groups * n == n_side and tile % (2 * SUBLANES) == 0 and side_ffn[3].shape[0] == 1
        tok = lambda w: pl.BlockSpec((1, tile, w), lambda g, j: (0, g * n + j, 0))
        ffn_specs, ffn_args, ffn_out = _ffn_operands(x_side, att_side, rec_side, *side_ffn[3:], tok,
                                                     lambda g, j: (0, 0, 0))
        in_specs += ffn_specs
        args += ffn_args
        out_specs.append(ffn_out)
        out_shape.append(jax.ShapeDtypeStruct(x_side.shape, F32))
        final_norm = side_ffn[-1] is not None
    outs = pl.pallas_call(
        functools.partial(_lru_fwd_kernel, tc=tc, side_ffn=final_norm),
        grid=(groups, n),
        in_specs=in_specs,
        out_specs=out_specs,
        out_shape=out_shape,
        scratch_shapes=[slabs(SCAN_PAD), carry],
        compiler_params=_cparams(("parallel", "arbitrary"), 56),
        name="lru_fwd" if side_ffn is None else "lru_fwd_side_ffn",
    )(*args)
    h_fin = jnp.stack([outs[1].reshape(b, c), hb_fin.reshape(b, c)], axis=1)
    if side_ffn is None:
        return outs[0], h_fin
    return outs[0], h_fin, outs[2].reshape(side_ffn[0].shape)


def _ffn_stages(x_ref, att_ref, rec_ref, mod_ref, g_ref, wo_ref, wg_ref, wu_ref, wd_ref, *rest, final_norm):
    o_ref = rest[-1]
    v = {}

    def project():
        mix = jnp.concatenate([att_ref[0], rec_ref[0]], axis=1)
        v['x1'] = x_ref[0] + mod_ref[0, 2:3, :] * jnp.dot(mix, wo_ref[...], preferred_element_type=F32)
        v['h2'] = (_rmsnorm(v['x1'], g_ref[...]) * (1.0 + mod_ref[0, 4:5, :]) + mod_ref[0, 3:4, :]).astype(BF16)

    def gate():
        g = jnp.dot(v['h2'], wg_ref[...], preferred_element_type=F32)
        v['gate'] = g * jax.nn.sigmoid(g)

    def up():
        v['act'] = (v.pop('gate') * jnp.dot(v.pop('h2'), wu_ref[...], preferred_element_type=F32)).astype(BF16)

    def down():
        x2 = v.pop('x1') + mod_ref[0, 5:6, :] * jnp.dot(v.pop('act'), wd_ref[...], preferred_element_type=F32)
        if final_norm:
            x2 = _rmsnorm(x2, rest[0][...])
        o_ref[0] = x2

    return project, gate, up, down


def _out_ffn_kernel(*refs, final_norm):
    for stage in _ffn_stages(*refs, final_norm=final_norm):
        stage()


def _ffn_operands(x, att, rec, mod, g2, w_out, w_gate, w_up, w_down, layer, g_final, tok, mod_map):
    d = x.shape[-1]
    d_ff = w_gate.shape[-1]
    in_specs = [
        tok(d), tok(ATT_WIDTH), tok(LRU_WIDTH),
        pl.BlockSpec((1, 6, d), mod_map),
        _resident((1, d)),
        _layer_resident((d, d), layer),
        _layer_resident((d, d_ff), layer),
        _layer_resident((d, d_ff), layer),
        _layer_resident((d_ff, d), layer),
    ]
    args = [x, att, rec, mod, g2, w_out, w_gate, w_up, w_down]
    if g_final is not None:
        in_specs.append(_resident((1, d)))
        args.append(g_final)
    return in_specs, args, tok(d)


def _out_ffn(x, att, rec, mod, g2, w_out, w_gate, w_up, w_down, layer, g_final=None):
    b, t, d = x.shape
    tm = min(512, t)
    shared = mod.shape[0] == 1
    mod_map = (lambda i, j: (0, 0, 0)) if shared else (lambda i, j: (i, 0, 0))
    tok = lambda w: pl.BlockSpec((1, tm, w), lambda i, j: (i, j, 0))
    in_specs, args, out_spec = _ffn_operands(x, att, rec, mod, g2, w_out, w_gate, w_up, w_down, layer, g_final,
                                             tok, mod_map)
    return pl.pallas_call(
        functools.partial(_out_ffn_kernel, final_norm=g_final is not None),
        grid=(b, t // tm),
        in_specs=in_specs,
        out_specs=out_spec,
        out_shape=jax.ShapeDtypeStruct((b, t, d), F32),
        compiler_params=_cparams(("parallel", "parallel"), 56),
        name="out_ffn",
    )(*args)


def _block_diag_halves(w):
    per_half = LRU_BLOCKS // 2
    w4 = w.reshape(2, per_half, LRU_BLOCK, LRU_BLOCK)
    eye = jnp.eye(per_half, dtype=w.dtype)
    return jnp.einsum('hnjk,nm->hnjmk', w4, eye).reshape(2, per_half * LRU_BLOCK, per_half * LRU_BLOCK)


def _head_pair_layout(cache):
    b, depth, p = cache.shape[:3]
    return cache.reshape(b, depth, p, HEAD_PAIRS, LANES).transpose(1, 0, 3, 2, 4).astype(BF16)


def kernel(x_prompt, x_sample, cache_k, cache_v, state_lru, c, c_ctx, w_mod, b_mod, norm1, norm2, w_in, w_out, rpb, conv_w, conv_b, lru_a, lru_wr, lru_br, lru_wi, lru_bi, w_gate, w_up, w_down, norm_final):
    depth = w_in.shape[0]
    n_ctx, n_lat = x_prompt.shape[0], x_sample.shape[0]
    d = x_prompt.shape[-1]
    assert n_lat + 1 <= MOD_ROWS

    cvec = jnp.zeros((MOD_ROWS, d), F32).at[:n_lat].set(c).at[n_lat].set(c_ctx)
    mod = _modulation(cvec, w_mod, b_mod).reshape(depth, MOD_ROWS, 6, d)
    t2 = _bias_tables(rpb)
    ck, cv = _head_pair_layout(cache_k), _head_pair_layout(cache_v)

    w_in_b, w_out_b = w_in.astype(BF16), w_out.astype(BF16)
    w_gate_b, w_up_b, w_down_b = w_gate.astype(BF16), w_up.astype(BF16), w_down.astype(BF16)
    wg = jnp.stack([
        jnp.stack([jnp.concatenate([_block_diag_halves(lru_wr[l, dr]), _block_diag_halves(lru_wi[l, dr])], axis=-1)
                   for dr in range(2)]) for l in range(depth)]).astype(BF16)
    bg = 0.5 * jnp.stack([lru_br, lru_bi], axis=2)
    conv_w_half, conv_b_half = 0.5 * conv_w, 0.5 * conv_b
    h0_ctx = jnp.zeros((n_ctx, 2, LRU_WIDTH), F32)

    def mixers(x, l, mod_l, attend, kv_all=None):
        outs = _in_proj(x, mod_l, norm1[l][None], w_in_b, l, kv_all)
        q, k, v, xb, gy = outs[:5]
        return attend(q, k, v), (xb, gy, conv_w_half[l], conv_b_half[l][None], wg[l], bg[l], lru_a[l]), outs[5:]

    xp, xs = x_prompt, x_sample
    kv_all = [jnp.zeros((n_ctx, depth, x_prompt.shape[1], ATT_WIDTH), F32) for _ in range(2)]
    hs_out = []
    for l in range(depth):
        g_final = norm_final[None] if l == depth - 1 else None
        ffn_weights = (norm2[l][None], w_out_b, w_gate_b, w_up_b, w_down_b, l, g_final)
        mod_ctx = mod[l, n_lat:n_lat + 1]
        att_p, lru_p, kv_all = mixers(xp, l, mod_ctx, _ctx_attention, kv_all)
        rec_p, h_l = _lru(*lru_p, h0_ctx)
        hs_out.append(h_l)
        mod_lat = mod[l, :n_lat]
        attend_lat = functools.partial(_lat_attention, kc=ck[l], vc=cv[l], t2=t2[l])
        att_s, lru_s, _ = mixers(xs, l, mod_lat, attend_lat)
        rec_s, _, xp = _lru(*lru_s, state_lru[:, l], side_ffn=(xp, att_p, rec_p, mod_ctx) + ffn_weights)
        xs = _out_ffn(xs, att_s, rec_s, mod_lat, *ffn_weights)
    new_k, new_v = (a.reshape(n_ctx, depth, -1, N_ATT_HEADS, HEAD_DIM) for a in kv_all)
    return (xp, xs, new_k, new_v, jnp.stack(hs_out, axis=1))
```

```python
import functools

import jax
import jax.numpy as jnp
from jax import lax
from jax.experimental import pallas as pl
from jax.experimental.pallas import tpu as pltpu

F32 = jnp.float32
BF16 = jnp.bfloat16

D_MODEL = 1024
GRID_W = 64
N_ATT_HEADS = 8
HEAD_DIM = 64
ATT_WIDTH = N_ATT_HEADS * HEAD_DIM
ATT_SCALE = HEAD_DIM ** -0.5
WIN_ROWS = 8
WIN_COLS = 16
LRU_WIDTH = D_MODEL // 2
LRU_BLOCKS = 8
LRU_BLOCK = LRU_WIDTH // LRU_BLOCKS
CONV_W = 4
CONV_LEFT = 2
LRU_C = 8.0
IN_COLS = 3 * ATT_WIDTH + 2 * LRU_WIDTH
EPS = 1e-6
NEG_INF = -1e30
F32_TINY = float(jnp.finfo(jnp.float32).tiny)
LOG2_E = 1.4426950408889634
Q_SCALE = ATT_SCALE * LOG2_E

LANES = 128
SUBLANES = 8
HEAD_PAIRS = ATT_WIDTH // LANES
N_ROW_IDX = 2 * WIN_ROWS - 1
N_COL_IDX = 2 * WIN_COLS - 1
N_BIAS_TILES = N_ROW_IDX - 1
MOD_ROWS = 16
LRU_GROUP = SUBLANES
SCAN_PAD = SUBLANES
X_PAD = 3 * SUBLANES
LRU_TIME_BLOCK = 32
MIB = 1024 * 1024


def _cparams(semantics, vmem_mib):
    return pltpu.CompilerParams(dimension_semantics=semantics, vmem_limit_bytes=vmem_mib * MIB)


def _resident(shape):
    zeros = (0,) * len(shape)
    return pl.BlockSpec(shape, lambda *_: zeros, pipeline_mode=pl.Buffered(1))


def _layer_resident(shape, layer):
    index = (layer,) + (0,) * len(shape)
    return pl.BlockSpec((None,) + tuple(shape), lambda *_: index, pipeline_mode=pl.Buffered(1))


def _rmsnorm(x, g):
    return (x * lax.rsqrt(jnp.mean(x * x, axis=-1, keepdims=True) + EPS)) * g


def _mod_kernel(c_ref, w_ref, b_ref, o_ref):
    cv = c_ref[...]
    s = (cv * jax.nn.sigmoid(cv)).astype(BF16)
    o_ref[0] = jnp.dot(s, w_ref[0].astype(BF16), preferred_element_type=F32) + b_ref[0]


def _modulation(cvec, w_mod, b_mod):
    depth, d, n = w_mod.shape
    tn = n // 4
    return pl.pallas_call(
        _mod_kernel,
        grid=(depth, n // tn),
        in_specs=[
            pl.BlockSpec((MOD_ROWS, d), lambda l, j: (0, 0)),
            pl.BlockSpec((1, d, tn), lambda l, j: (l, 0, j)),
            pl.BlockSpec((1, 1, tn), lambda l, j: (l, 0, j)),
        ],
        out_specs=pl.BlockSpec((1, MOD_ROWS, tn), lambda l, j: (l, 0, j)),
        out_shape=jax.ShapeDtypeStruct((depth, MOD_ROWS, n), F32),
        compiler_params=_cparams(("arbitrary", "arbitrary"), 40),
        name="adaln_mod",
    )(cvec, w_mod, b_mod.reshape(depth, 1, n))


def _bias_table_kernel(rpb_ref, o_ref):
    shape = (GRID_W, LANES)
    c = lax.broadcasted_iota(jnp.int32, shape, 0)
    lane = lax.broadcasted_iota(jnp.int32, shape, 1)
    upper = lane >= GRID_W
    kc = jnp.where(upper, lane - GRID_W, lane)
    cs = jnp.clip(c - WIN_COLS // 2, 0, GRID_W - WIN_COLS)
    ok = (kc >= cs) & (kc < cs + WIN_COLS)
    lo_shift = LANES - (WIN_COLS - 1)
    hi_shift = (lo_shift + GRID_W) % LANES
    for h in range(2):
        for i in range(N_BIAS_TILES):
            lo = pltpu.roll(jnp.broadcast_to(rpb_ref[0, h, i:i + 1, :], shape), lo_shift, 1, stride=1, stride_axis=0)
            hi = pltpu.roll(jnp.broadcast_to(rpb_ref[0, h, i + 1:i + 2, :], shape), hi_shift, 1, stride=1, stride_axis=0)
            o_ref[0, i, h * GRID_W:(h + 1) * GRID_W, :] = jnp.where(ok, jnp.where(upper, hi, lo) * LOG2_E, NEG_INF)


def _bias_tables(rpb):
    depth = rpb.shape[0]
    n = depth * HEAD_PAIRS
    padded = jnp.pad(rpb, ((0, 0), (0, 0), (0, 2 * SUBLANES - N_ROW_IDX), (0, LANES - N_COL_IDX)))
    out = pl.pallas_call(
        _bias_table_kernel,
        grid=(n,),
        in_specs=[pl.BlockSpec((1, 2, 2 * SUBLANES, LANES), lambda i: (i, 0, 0, 0))],
        out_specs=pl.BlockSpec((1, N_BIAS_TILES, LANES, LANES), lambda i: (i, 0, 0, 0)),
        out_shape=jax.ShapeDtypeStruct((n, N_BIAS_TILES, LANES, LANES), F32),
        compiler_params=_cparams(("arbitrary",), 16),
        name="bias_tables",
    )(padded.reshape(n, 2, 2 * SUBLANES, LANES))
    return out.reshape(depth, HEAD_PAIRS, N_BIAS_TILES, LANES, LANES)


def _in_proj_kernel(x_ref, mod_ref, g_ref, w_ref, q_ref, k_ref, v_ref, xb_ref, gy_ref, *kv_f32_refs):
    h = _rmsnorm(x_ref[0], g_ref[...]) * (1.0 + mod_ref[0, 1:2, :]) + mod_ref[0, 0:1, :]
    p = jnp.dot(h.astype(BF16), w_ref[...], preferred_element_type=F32)
    for j in range(HEAD_PAIRS):
        lo = j * LANES
        q_ref[0, j] = (p[:, lo:lo + LANES] * Q_SCALE).astype(BF16)
        k_ref[0, j] = p[:, ATT_WIDTH + lo:ATT_WIDTH + lo + LANES].astype(BF16)
        v_ref[0, j] = p[:, 2 * ATT_WIDTH + lo:2 * ATT_WIDTH + lo + LANES].astype(BF16)
    xb_ref[0] = p[:, 3 * ATT_WIDTH:3 * ATT_WIDTH + LRU_WIDTH]
    gy_ref[0] = jax.nn.gelu(p[:, 3 * ATT_WIDTH + LRU_WIDTH:])
    if kv_f32_refs:
        kf_ref, vf_ref = kv_f32_refs
        kf_ref[0] = p[:, ATT_WIDTH:2 * ATT_WIDTH]
        vf_ref[0] = p[:, 2 * ATT_WIDTH:3 * ATT_WIDTH]


def _in_proj_kv_kernel(x_ref, mod_ref, g_ref, w_ref, k_all_ref, v_all_ref, *out_refs):
    del k_all_ref, v_all_ref
    _in_proj_kernel(x_ref, mod_ref, g_ref, w_ref, *out_refs)


def _in_proj(x, mod, g, w, layer, kv_all=None):
    b, t, d = x.shape
    tm = min(1024, t)
    shared = mod.shape[0] == 1
    mod_map = (lambda i, j: (0, 0, 0)) if shared else (lambda i, j: (i, 0, 0))
    heads_spec = pl.BlockSpec((1, HEAD_PAIRS, tm, LANES), lambda i, j: (i, 0, j, 0))
    heads_shape = jax.ShapeDtypeStruct((b, HEAD_PAIRS, t, LANES), BF16)
    half_spec = pl.BlockSpec((1, tm, LRU_WIDTH), lambda i, j: (i, j, 0))
    half_shape = jax.ShapeDtypeStruct((b, t, LRU_WIDTH), F32)
    in_specs = [
        pl.BlockSpec((1, tm, d), lambda i, j: (i, j, 0)),
        pl.BlockSpec((1, 6, d), mod_map),
        _resident((1, d)),
        _layer_resident((d, IN_COLS), layer),
    ]
    out_specs = [heads_spec] * 3 + [half_spec] * 2
    out_shape = [heads_shape] * 3 + [half_shape] * 2
    args = [x, mod, g, w]
    aliases = {}
    if kv_all is not None:
        layer_spec = pl.BlockSpec((1, None, tm, ATT_WIDTH), lambda i, j: (i, layer, j, 0))
        for arr in kv_all:
            aliases[len(args)] = len(out_specs)
            in_specs.append(pl.BlockSpec(memory_space=pl.ANY))
            out_specs.append(layer_spec)
            out_shape.append(jax.ShapeDtypeStruct(arr.shape, arr.dtype))
            args.append(arr)
    return pl.pallas_call(
        _in_proj_kernel if kv_all is None else _in_proj_kv_kernel,
        grid=(b, t // tm),
        in_specs=in_specs,
        out_specs=out_specs,
        out_shape=out_shape,
        input_output_aliases=aliases,
        compiler_params=_cparams(("parallel", "parallel"), 48),
        name="in_proj",
    )(*args)


def _head_masks(shape):
    lane = lax.broadcasted_iota(jnp.int32, shape, 1)
    return lane < HEAD_DIM, lane >= HEAD_DIM


def _scores(qh, k):
    return lax.dot_general(qh, k, (((1,), (1,)), ((), ())), preferred_element_type=F32)


def _ctx_attn_kernel(q_ref, k_ref, v_ref, o_ref):
    t = q_ref.shape[2]
    own_lanes = ((lax.broadcasted_iota(jnp.int32, (2 * t, LANES), 0) < t)
                 == (lax.broadcasted_iota(jnp.int32, (2 * t, LANES), 1) < HEAD_DIM))
    lo_lanes, _ = _head_masks((t, LANES))
    scores = []
    for hp in range(HEAD_PAIRS):
        q2 = jnp.concatenate([q_ref[0, hp], q_ref[0, hp]], axis=0)
        scores.append(_scores(jnp.where(own_lanes, q2, jnp.zeros_like(q2)), k_ref[0, hp]))
    for hp, s in enumerate(scores):
        e = jnp.exp2(s - jnp.max(s, axis=-1, keepdims=True))
        l = jnp.sum(e, axis=-1, keepdims=True)
        o2 = jnp.dot(e.astype(BF16), v_ref[0, hp], preferred_element_type=F32) / l
        o_ref[0, :, hp * LANES:(hp + 1) * LANES] = jnp.where(lo_lanes, o2[:t], o2[t:]).astype(BF16)


def _ctx_attention(q, k, v):
    b, _, t, _ = q.shape
    spec = pl.BlockSpec((1, HEAD_PAIRS, t, LANES), lambda i: (i, 0, 0, 0))
    return pl.pallas_call(
        _ctx_attn_kernel,
        grid=(b,),
        in_specs=[spec] * 3,
        out_specs=pl.BlockSpec((1, t, ATT_WIDTH), lambda i: (i, 0, 0)),
        out_shape=jax.ShapeDtypeStruct((b, t, ATT_WIDTH), BF16),
        compiler_params=_cparams(("parallel",), 32),
        name="ctx_attention",
    )(q, k, v)


def _lat_attn_kernel(q_ref, k_ref, v_ref, kc_ref, vc_ref, t2_ref, o_ref, *, rows_per_step, n_rows):
    rb = pl.program_id(2)
    kc, vc = kc_ref[0, 0], vc_ref[0, 0]
    n_loc = WIN_ROWS * GRID_W
    shape2 = (2 * GRID_W, LANES)
    own_lanes = ((lax.broadcasted_iota(jnp.int32, shape2, 0) < GRID_W)
                 == (lax.broadcasted_iota(jnp.int32, shape2, 1) < HEAD_DIM))
    lo_lanes, _ = _head_masks((GRID_W, LANES))

    def scores(i):
        r = rb * rows_per_step + i
        rs = jnp.clip(r - WIN_ROWS // 2, 0, n_rows - WIN_ROWS)
        off = rs - r + WIN_ROWS - 1
        q = q_ref[0, 0, i * GRID_W:(i + 1) * GRID_W, :]
        q2 = jnp.concatenate([q, q], axis=0)
        q2 = jnp.where(own_lanes, q2, jnp.zeros_like(q2))
        k_start = pl.multiple_of(rs * GRID_W, GRID_W)
        bias = jnp.concatenate([t2_ref[0, off + 2 * j] for j in range(WIN_ROWS // 2)], axis=1)
        s_loc = _scores(q2, k_ref[0, 0, pl.ds(k_start, n_loc), :]) + bias
        s_ctx = _scores(q2, kc)
        return jnp.concatenate([s_loc, s_ctx], axis=1), k_start

    def finish(i, s, m, k_start):
        e = jnp.exp2(s - m)
        l = jnp.sum(e, axis=-1, keepdims=True)
        e = e.astype(BF16)
        o2 = (jnp.dot(e[:, :n_loc], v_ref[0, 0, pl.ds(k_start, n_loc), :], preferred_element_type=F32)
              + jnp.dot(e[:, n_loc:], vc, preferred_element_type=F32)) / l
        o_ref[0, i * GRID_W:(i + 1) * GRID_W, :] = jnp.where(lo_lanes, o2[:GRID_W], o2[GRID_W:]).astype(BF16)

    rows = [scores(0), scores(1)] + [None] * rows_per_step
    maxima = [jnp.max(rows[0][0], axis=-1, keepdims=True)] + [None] * rows_per_step
    for i in range(rows_per_step):
        if i + 2 < rows_per_step:
            rows[i + 2] = scores(i + 2)
        if i + 1 < rows_per_step:
            maxima[i + 1] = jnp.max(rows[i + 1][0], axis=-1, keepdims=True)
        finish(i, rows[i][0], maxima[i], rows[i][1])
        rows[i] = maxima[i] = None


def _lat_attention(q, k, v, kc, vc, t2):
    b, _, t, _ = q.shape
    lc = kc.shape[2]
    n_rows = t // GRID_W
    rows_per_step = min(32, n_rows)
    tq = rows_per_step * GRID_W
    full = pl.BlockSpec((1, 1, t, LANES), lambda i, j, r: (i, j, 0, 0))
    ctx = pl.BlockSpec((1, 1, lc, LANES), lambda i, j, r: (i, j, 0, 0))
    return pl.pallas_call(
        functools.partial(_lat_attn_kernel, rows_per_step=rows_per_step, n_rows=n_rows),
        grid=(b, HEAD_PAIRS, n_rows // rows_per_step),
        in_specs=[
            pl.BlockSpec((1, 1, tq, LANES), lambda i, j, r: (i, j, r, 0)),
            full, full, ctx, ctx,
            pl.BlockSpec((1, N_BIAS_TILES, LANES, LANES), lambda i, j, r: (j, 0, 0, 0)),
        ],
        out_specs=pl.BlockSpec((1, tq, LANES), lambda i, j, r: (i, r, j)),
        out_shape=jax.ShapeDtypeStruct((b, t, ATT_WIDTH), BF16),
        compiler_params=_cparams(("parallel", "parallel", "arbitrary"), 32),
        name="lat_attention",
    )(q, k, v, kc, vc, t2)


def _log_sigmoid(x):
    return jnp.minimum(x, 0.0) - jnp.log1p(jnp.exp(-jnp.abs(x)))


def _gate_matmuls(xh, wg_ref):
    half = LRU_WIDTH // 2
    xhb = xh.astype(BF16)
    return [jnp.dot(xhb[:, hf * half:(hf + 1) * half], wg_ref[hf], preferred_element_type=F32) for hf in range(2)]


def _lru_coefficients(pre, xh, bg_ref, neg_log_sig4, neg_log2_sig4):
    half = LRU_WIDTH // 2
    t_r = jnp.tanh(jnp.concatenate([pre[0][:, :half], pre[1][:, :half]], axis=1) + bg_ref[0:1, :])
    t_i = jnp.tanh(jnp.concatenate([pre[0][:, half:], pre[1][:, half:]], axis=1) + bg_ref[1:2, :])
    gate = t_r + 1.0
    a = jnp.exp2(gate * neg_log2_sig4)
    z = jnp.tanh(gate * neg_log_sig4) * (1.0 + a * a)
    root = z * lax.rsqrt(jnp.maximum(z, F32_TINY))
    return a, (root * (t_i + 1.0)) * xh


def _pipelined_recurrence(blocks, gate_inputs, xh_block, bg_ref, scales, h, recur):
    m = LRU_TIME_BLOCK * LRU_GROUP
    pre = gate_inputs(blocks[0])
    for pos, blk in enumerate(blocks):
        following = gate_inputs(blocks[pos + 1]) if pos + 1 < len(blocks) else None
        a, b = _lru_coefficients(pre, xh_block(slice(blk * m, (blk + 1) * m)), bg_ref, *scales)
        h = recur(blk, a, b, h)
        pre = following
    return h


def _log_sigmoid_scales(la_ref):
    neg_log_sig4 = (-0.5 * LRU_C) * _log_sigmoid(la_ref[...])
    return neg_log_sig4, neg_log_sig4 * (-LOG2_E)


def _lru_bwd_kernel(xb_ref, xprev_ref, xnext_ref, cw_ref, cb_ref, wg_ref, bg_ref, la_ref, h0_ref,
                    xh_ref, hb_ref, hfin_ref, x_s, carry_s, *, tc):
    j = pl.program_id(1)
    n = pl.num_programs(1)
    cj = n - 1 - j
    x_pitch = tc + X_PAD
    n_slabs = LRU_WIDTH // LANES
    lane_slab = lambda v, c: v[:, c * LANES:(c + 1) * LANES]

    @pl.when(j == 0)
    def _():
        carry_s[...] = h0_ref[0]

    def park(bi, carry):
        base = pl.multiple_of(bi * x_pitch, SUBLANES)
        pieces = ((0, jnp.where(cj > 0, xprev_ref[bi], 0.0)), (SUBLANES, xb_ref[bi]),
                  (SUBLANES + tc, jnp.where(cj < n - 1, xnext_ref[bi], 0.0)))
        for start, v in pieces:
            for c in range(n_slabs):
                x_s[c, pl.ds(base + start, v.shape[0]), :] = lane_slab(v, c)
        return carry

    lax.fori_loop(0, LRU_GROUP, park, 0)

    scales = _log_sigmoid_scales(la_ref)

    def time_rows(t):
        rows = pl.ds(SUBLANES + t, LRU_GROUP, stride=x_pitch)
        return jnp.concatenate([x_s[c, rows, :] for c in range(n_slabs)], axis=1)

    m = LRU_TIME_BLOCK * LRU_GROUP

    def conv_block(blk):
        t0 = blk * LRU_TIME_BLOCK
        xw = jnp.concatenate([time_rows(t0 + k) for k in range(-CONV_LEFT, LRU_TIME_BLOCK + CONV_W - 1 - CONV_LEFT)],
                             axis=0)
        xh = cb_ref[...]
        for tap in range(CONV_W):
            xh = xh + cw_ref[tap:tap + 1, :] * xw[tap * LRU_GROUP:tap * LRU_GROUP + m]
        xh_ref[0, blk * m:(blk + 1) * m, :] = xh
        return _gate_matmuls(xh, wg_ref)

    def recur(blk, a, b, h):
        for i in reversed(range(LRU_TIME_BLOCK)):
            rows = slice(i * LRU_GROUP, (i + 1) * LRU_GROUP)
            h = a[rows] * h + b[rows]
            hb_ref[0, blk * m + i * LRU_GROUP:blk * m + (i + 1) * LRU_GROUP, :] = h
        return h

    blocks = list(reversed(range(tc // LRU_TIME_BLOCK)))
    carry_s[...] = _pipelined_recurrence(blocks, conv_block, lambda rows: xh_ref[0, rows, :], bg_ref, scales,
                                         carry_s[...], recur)

    @pl.when(j == n - 1)
    def _():
        hfin_ref[0] = carry_s[...]


def _lru_fwd_kernel(xh_ref, hb_ref, gy_ref, wg_ref, bg_ref, la_ref, h0_ref, out_ref, hfin_ref,
                    s_s, carry_s, *, tc):
    j = pl.program_id(1)
    n = pl.num_programs(1)
    pitch = tc + SCAN_PAD
    n_slabs = LRU_WIDTH // LANES
    m = LRU_TIME_BLOCK * LRU_GROUP

    @pl.when(j == 0)
    def _():
        carry_s[...] = h0_ref[0]

    scales = _log_sigmoid_scales(la_ref)

    def recur(blk, a, b, h):
        for i in range(LRU_TIME_BLOCK):
            rows = slice(i * LRU_GROUP, (i + 1) * LRU_GROUP)
            h = a[rows] * h + b[rows]
            both = h + hb_ref[0, blk * m + i * LRU_GROUP:blk * m + (i + 1) * LRU_GROUP, :]
            for c in range(n_slabs):
                s_s[c, pl.ds(blk * LRU_TIME_BLOCK + i, LRU_GROUP, stride=pitch), :] = both[:, c * LANES:(c + 1) * LANES]
        return h

    xh_block = lambda rows: xh_ref[0, rows, :]
    carry_s[...] = _pipelined_recurrence(
        list(range(tc // LRU_TIME_BLOCK)),
        lambda blk: _gate_matmuls(xh_block(slice(blk * m, (blk + 1) * m)), wg_ref),
        xh_block, bg_ref, scales, carry_s[...], recur)

    def emit(bi, carry):
        rows = pl.ds(pl.multiple_of(bi * pitch, SUBLANES), tc)
        both = jnp.concatenate([s_s[c, rows, :] for c in range(n_slabs)], axis=1)
        out_ref[bi] = (both * gy_ref[bi]).astype(BF16)
        return carry

    lax.fori_loop(0, LRU_GROUP, emit, 0)

    @pl.when(j == n - 1)
    def _():
        hfin_ref[0] = carry_s[...]


def _lru(xb, gy, conv_w, conv_b, wg, bg, la, h0):
    b, t, c = xb.shape
    assert b % LRU_GROUP == 0
    groups = b // LRU_GROUP
    tc = min(256, t)
    n = t // tc
    per = tc // SUBLANES
    assert tc % (2 * SUBLANES) == 0 and tc % LRU_TIME_BLOCK == 0
    h0 = h0.reshape(groups, LRU_GROUP, 2, c)
    state = pl.BlockSpec((1, LRU_GROUP, c), lambda g, j: (g, 0, 0))
    state_shape = jax.ShapeDtypeStruct((groups, LRU_GROUP, c), F32)
    tm_rows = tc * LRU_GROUP
    tm_shape = jax.ShapeDtypeStruct((groups, t * LRU_GROUP, c), F32)
    slabs = lambda pad: pltpu.VMEM((c // LANES, LRU_GROUP * (tc + pad), LANES), F32)
    carry = pltpu.VMEM((LRU_GROUP, c), F32)
    gate_specs = [_resident((2, c // 2, c)), _resident((2, c)), _resident((1, c)), state]

    rev = lambda j: n - 1 - j
    rev_tm = pl.BlockSpec((1, tm_rows, c), lambda g, j: (g, rev(j), 0))
    halo = lambda f: pl.BlockSpec((LRU_GROUP, SUBLANES, c), lambda g, j: (g, f(rev(j)), 0))
    xh, hb, hb_fin = pl.pallas_call(
        functools.partial(_lru_bwd_kernel, tc=tc),
        grid=(groups, n),
        in_specs=[
            pl.BlockSpec((LRU_GROUP, tc, c), lambda g, j: (g, rev(j), 0)),
            halo(lambda cj: jnp.maximum(cj * per - 1, 0)),
            halo(lambda cj: jnp.minimum((cj + 1) * per, t // SUBLANES - 1)),
            _resident((CONV_W, c)),
            _resident((1, c)),
        ] + gate_specs,
        out_specs=[rev_tm, rev_tm, state],
        out_shape=[tm_shape, tm_shape, state_shape],
        scratch_shapes=[slabs(X_PAD), carry],
        compiler_params=_cparams(("parallel", "arbitrary"), 56),
        name="lru_bwd",
    )(xb, xb, xb, conv_w, conv_b, wg[1], bg[1], la[1:2], h0[:, :, 1])

    fwd_tm = pl.BlockSpec((1, tm_rows, c), lambda g, j: (g, j, 0))
    chunk = pl.BlockSpec((LRU_GROUP, tc, c), lambda g, j: (g, j, 0))
    rec, hf_fin = pl.pallas_call(
        functools.partial(_lru_fwd_kernel, tc=tc),
        grid=(groups, n),
        in_specs=[fwd_tm, fwd_tm, chunk] + gate_specs,
        out_specs=[chunk, state],
        out_shape=[jax.ShapeDtypeStruct((b, t, c), BF16), state_shape],
        scratch_shapes=[slabs(SCAN_PAD), carry],
        compiler_params=_cparams(("parallel", "arbitrary"), 56),
        name="lru_fwd",
    )(xh, hb, gy, wg[0], bg[0], la[0:1], h0[:, :, 0])
    return rec, jnp.stack([hf_fin.reshape(b, c), hb_fin.reshape(b, c)], axis=1)


def _out_ffn_kernel(x_ref, att_ref, rec_ref, mod_ref, g_ref, wo_ref, wg_ref, wu_ref, wd_ref, *rest, final_norm):
    o_ref = rest[-1]
    mix = jnp.concatenate([att_ref[0], rec_ref[0]], axis=1)
    x1 = x_ref[0] + mod_ref[0, 2:3, :] * jnp.dot(mix, wo_ref[...], preferred_element_type=F32)
    h2 = (_rmsnorm(x1, g_ref[...]) * (1.0 + mod_ref[0, 4:5, :]) + mod_ref[0, 3:4, :]).astype(BF16)
    gate = jnp.dot(h2, wg_ref[...], preferred_element_type=F32)
    up = jnp.dot(h2, wu_ref[...], preferred_element_type=F32)
    act = ((gate * jax.nn.sigmoid(gate)) * up).astype(BF16)
    x2 = x1 + mod_ref[0, 5:6, :] * jnp.dot(act, wd_ref[...], preferred_element_type=F32)
    if final_norm:
        x2 = _rmsnorm(x2, rest[0][...])
    o_ref[0] = x2


def _out_ffn(x, att, rec, mod, g2, w_out, w_gate, w_up, w_down, layer, g_final=None):
    b, t, d = x.shape
    d_ff = w_gate.shape[-1]
    tm = min(512, t)
    shared = mod.shape[0] == 1
    mod_map = (lambda i, j: (0, 0, 0)) if shared else (lambda i, j: (i, 0, 0))
    tok = lambda w: pl.BlockSpec((1, tm, w), lambda i, j: (i, j, 0))
    in_specs = [
        tok(d), tok(ATT_WIDTH), tok(LRU_WIDTH),
        pl.BlockSpec((1, 6, d), mod_map),
        _resident((1, d)),
        _layer_resident((d, d), layer),
        _layer_resident((d, d_ff), layer),
        _layer_resident((d, d_ff), layer),
        _layer_resident((d_ff, d), layer),
    ]
    args = [x, att, rec, mod, g2, w_out, w_gate, w_up, w_down]
    if g_final is not None:
        in_specs.append(_resident((1, d)))
        args.append(g_final)
    return pl.pallas_call(
        functools.partial(_out_ffn_kernel, final_norm=g_final is not None),
        grid=(b, t // tm),
        in_specs=in_specs,
        out_specs=tok(d),
        out_shape=jax.ShapeDtypeStruct((b, t, d), F32),
        compiler_params=_cparams(("parallel", "parallel"), 56),
        name="out_ffn",
    )(*args)


def _block_diag_halves(w):
    per_half = LRU_BLOCKS // 2
    w4 = w.reshape(2, per_half, LRU_BLOCK, LRU_BLOCK)
    eye = jnp.eye(per_half, dtype=w.dtype)
    return jnp.einsum('hnjk,nm->hnjmk', w4, eye).reshape(2, per_half * LRU_BLOCK, per_half * LRU_BLOCK)


def _head_pair_layout(cache):
    b, depth, p = cache.shape[:3]
    return cache.reshape(b, depth, p, HEAD_PAIRS, LANES).transpose(1, 0, 3, 2, 4).astype(BF16)


def kernel(x_prompt, x_sample, cache_k, cache_v, state_lru, c, c_ctx, w_mod, b_mod, norm1, norm2, w_in, w_out, rpb, conv_w, conv_b, lru_a, lru_wr, lru_br, lru_wi, lru_bi, w_gate, w_up, w_down, norm_final):
    depth = w_in.shape[0]
    n_ctx, n_lat = x_prompt.shape[0], x_sample.shape[0]
    d = x_prompt.shape[-1]
    assert n_lat + 1 <= MOD_ROWS

    cvec = jnp.zeros((MOD_ROWS, d), F32).at[:n_lat].set(c).at[n_lat].set(c_ctx)
    mod = _modulation(cvec, w_mod, b_mod).reshape(depth, MOD_ROWS, 6, d)
    t2 = _bias_tables(rpb)
    ck, cv = _head_pair_layout(cache_k), _head_pair_layout(cache_v)

    w_in_b, w_out_b = w_in.astype(BF16), w_out.astype(BF16)
    w_gate_b, w_up_b, w_down_b = w_gate.astype(BF16), w_up.astype(BF16), w_down.astype(BF16)
    wg = jnp.stack([
        jnp.stack([jnp.concatenate([_block_diag_halves(lru_wr[l, dr]), _block_diag_halves(lru_wi[l, dr])], axis=-1)
                   for dr in range(2)]) for l in range(depth)]).astype(BF16)
    bg = 0.5 * jnp.stack([lru_br, lru_bi], axis=2)
    conv_w_half, conv_b_half = 0.5 * conv_w, 0.5 * conv_b
    h0_ctx = jnp.zeros((n_ctx, 2, LRU_WIDTH), F32)

    def layer(x, l, mod_l, attend, h0, last, kv_all=None):
        outs = _in_proj(x, mod_l, norm1[l][None], w_in_b, l, kv_all)
        q, k, v, xb, gy = outs[:5]
        att = attend(q, k, v)
        rec, h_fin = _lru(xb, gy, conv_w_half[l], conv_b_half[l][None], wg[l], bg[l], lru_a[l], h0)
        x = _out_ffn(x, att, rec, mod_l, norm2[l][None], w_out_b, w_gate_b, w_up_b, w_down_b, l,
                     g_final=norm_final[None] if last else None)
        return x, outs[5:], h_fin

    xp, xs = x_prompt, x_sample
    kv_all = [jnp.zeros((n_ctx, depth, x_prompt.shape[1], ATT_WIDTH), F32) for _ in range(2)]
    hs_out = []
    for l in range(depth):
        last = l == depth - 1
        xp, kv_all, h_l = layer(xp, l, mod[l, n_lat:n_lat + 1], _ctx_attention, h0_ctx, last, kv_all)
        hs_out.append(h_l)
        attend_lat = functools.partial(_lat_attention, kc=ck[l], vc=cv[l], t2=t2[l])
        xs, _, _ = layer(xs, l, mod[l, :n_lat], attend_lat, state_lru[:, l], last)
    new_k, new_v = (a.reshape(n_ctx, depth, -1, N_ATT_HEADS, HEAD_DIM) for a in kv_all)
    return (xp, xs, new_k, new_v, jnp.stack(hs_out, axis=1))
```

```python
import functools

import jax
import jax.numpy as jnp
from jax import lax
from jax.experimental import pallas as pl
from jax.experimental.pallas import tpu as pltpu

F32 = jnp.float32
BF16 = jnp.bfloat16

D_MODEL = 1024
GRID_W = 64
N_ATT_HEADS = 8
HEAD_DIM = 64
ATT_WIDTH = N_ATT_HEADS * HEAD_DIM
ATT_SCALE = HEAD_DIM ** -0.5
WIN_ROWS = 8
WIN_COLS = 16
LRU_WIDTH = D_MODEL // 2
LRU_BLOCKS = 8
LRU_BLOCK = LRU_WIDTH // LRU_BLOCKS
CONV_W = 4
CONV_LEFT = 2
LRU_C = 8.0
IN_COLS = 3 * ATT_WIDTH + 2 * LRU_WIDTH
EPS = 1e-6
NEG_INF = -1e30
F32_TINY = float(jnp.finfo(jnp.float32).tiny)
LOG2_E = 1.4426950408889634
Q_SCALE = ATT_SCALE * LOG2_E

LANES = 128
SUBLANES = 8
HEAD_PAIRS = ATT_WIDTH // LANES
N_ROW_IDX = 2 * WIN_ROWS - 1
N_COL_IDX = 2 * WIN_COLS - 1
N_BIAS_TILES = N_ROW_IDX - 1
MOD_ROWS = 16
LRU_GROUP = SUBLANES
SCAN_PAD = SUBLANES
X_PAD = 3 * SUBLANES
LRU_TIME_BLOCK = 32
MIB = 1024 * 1024


def _cparams(semantics, vmem_mib):
    return pltpu.CompilerParams(dimension_semantics=semantics, vmem_limit_bytes=vmem_mib * MIB)


def _resident(shape):
    zeros = (0,) * len(shape)
    return pl.BlockSpec(shape, lambda *_: zeros, pipeline_mode=pl.Buffered(1))


def _layer_resident(shape, layer):
    index = (layer,) + (0,) * len(shape)
    return pl.BlockSpec((None,) + tuple(shape), lambda *_: index, pipeline_mode=pl.Buffered(1))


def _rmsnorm(x, g):
    return (x * lax.rsqrt(jnp.mean(x * x, axis=-1, keepdims=True) + EPS)) * g


def _mod_kernel(c_ref, w_ref, b_ref, o_ref):
    cv = c_ref[...]
    s = (cv * jax.nn.sigmoid(cv)).astype(BF16)
    o_ref[0] = jnp.dot(s, w_ref[0].astype(BF16), preferred_element_type=F32) + b_ref[0]


def _modulation(cvec, w_mod, b_mod):
    depth, d, n = w_mod.shape
    tn = n // 4
    return pl.pallas_call(
        _mod_kernel,
        grid=(depth, n // tn),
        in_specs=[
            pl.BlockSpec((MOD_ROWS, d), lambda l, j: (0, 0)),
            pl.BlockSpec((1, d, tn), lambda l, j: (l, 0, j)),
            pl.BlockSpec((1, 1, tn), lambda l, j: (l, 0, j)),
        ],
        out_specs=pl.BlockSpec((1, MOD_ROWS, tn), lambda l, j: (l, 0, j)),
        out_shape=jax.ShapeDtypeStruct((depth, MOD_ROWS, n), F32),
        compiler_params=_cparams(("arbitrary", "arbitrary"), 40),
        name="adaln_mod",
    )(cvec, w_mod, b_mod.reshape(depth, 1, n))


def _bias_table_kernel(rpb_ref, o_ref):
    shape = (GRID_W, LANES)
    c = lax.broadcasted_iota(jnp.int32, shape, 0)
    lane = lax.broadcasted_iota(jnp.int32, shape, 1)
    upper = lane >= GRID_W
    kc = jnp.where(upper, lane - GRID_W, lane)
    cs = jnp.clip(c - WIN_COLS // 2, 0, GRID_W - WIN_COLS)
    ok = (kc >= cs) & (kc < cs + WIN_COLS)
    lo_shift = LANES - (WIN_COLS - 1)
    hi_shift = (lo_shift + GRID_W) % LANES
    for h in range(2):
        for i in range(N_BIAS_TILES):
            lo = pltpu.roll(jnp.broadcast_to(rpb_ref[0, h, i:i + 1, :], shape), lo_shift, 1, stride=1, stride_axis=0)
            hi = pltpu.roll(jnp.broadcast_to(rpb_ref[0, h, i + 1:i + 2, :], shape), hi_shift, 1, stride=1, stride_axis=0)
            o_ref[0, i, h * GRID_W:(h + 1) * GRID_W, :] = jnp.where(ok, jnp.where(upper, hi, lo) * LOG2_E, NEG_INF)


def _bias_tables(rpb):
    depth = rpb.shape[0]
    n = depth * HEAD_PAIRS
    padded = jnp.pad(rpb, ((0, 0), (0, 0), (0, 2 * SUBLANES - N_ROW_IDX), (0, LANES - N_COL_IDX)))
    out = pl.pallas_call(
        _bias_table_kernel,
        grid=(n,),
        in_specs=[pl.BlockSpec((1, 2, 2 * SUBLANES, LANES), lambda i: (i, 0, 0, 0))],
        out_specs=pl.BlockSpec((1, N_BIAS_TILES, LANES, LANES), lambda i: (i, 0, 0, 0)),
        out_shape=jax.ShapeDtypeStruct((n, N_BIAS_TILES, LANES, LANES), F32),
        compiler_params=_cparams(("arbitrary",), 16),
        name="bias_tables",
    )(padded.reshape(n, 2, 2 * SUBLANES, LANES))
    return out.reshape(depth, HEAD_PAIRS, N_BIAS_TILES, LANES, LANES)


def _in_proj_kernel(x_ref, mod_ref, g_ref, w_ref, q_ref, k_ref, v_ref, xb_ref, gy_ref, *kv_f32_refs):
    h = _rmsnorm(x_ref[0], g_ref[...]) * (1.0 + mod_ref[0, 1:2, :]) + mod_ref[0, 0:1, :]
    p = jnp.dot(h.astype(BF16), w_ref[...], preferred_element_type=F32)
    for j in range(HEAD_PAIRS):
        lo = j * LANES
        q_ref[0, j] = (p[:, lo:lo + LANES] * Q_SCALE).astype(BF16)
        k_ref[0, j] = p[:, ATT_WIDTH + lo:ATT_WIDTH + lo + LANES].astype(BF16)
        v_ref[0, j] = p[:, 2 * ATT_WIDTH + lo:2 * ATT_WIDTH + lo + LANES].astype(BF16)
    xb_ref[0] = p[:, 3 * ATT_WIDTH:3 * ATT_WIDTH + LRU_WIDTH]
    gy_ref[0] = jax.nn.gelu(p[:, 3 * ATT_WIDTH + LRU_WIDTH:])
    if kv_f32_refs:
        kf_ref, vf_ref = kv_f32_refs
        kf_ref[0] = p[:, ATT_WIDTH:2 * ATT_WIDTH]
        vf_ref[0] = p[:, 2 * ATT_WIDTH:3 * ATT_WIDTH]


def _in_proj_kv_kernel(x_ref, mod_ref, g_ref, w_ref, k_all_ref, v_all_ref, *out_refs):
    del k_all_ref, v_all_ref
    _in_proj_kernel(x_ref, mod_ref, g_ref, w_ref, *out_refs)


def _in_proj(x, mod, g, w, layer, kv_all=None):
    b, t, d = x.shape
    tm = min(1024, t)
    shared = mod.shape[0] == 1
    mod_map = (lambda i, j: (0, 0, 0)) if shared else (lambda i, j: (i, 0, 0))
    heads_spec = pl.BlockSpec((1, HEAD_PAIRS, tm, LANES), lambda i, j: (i, 0, j, 0))
    heads_shape = jax.ShapeDtypeStruct((b, HEAD_PAIRS, t, LANES), BF16)
    half_spec = pl.BlockSpec((1, tm, LRU_WIDTH), lambda i, j: (i, j, 0))
    half_shape = jax.ShapeDtypeStruct((b, t, LRU_WIDTH), F32)
    in_specs = [
        pl.BlockSpec((1, tm, d), lambda i, j: (i, j, 0)),
        pl.BlockSpec((1, 6, d), mod_map),
        _resident((1, d)),
        _layer_resident((d, IN_COLS), layer),
    ]
    out_specs = [heads_spec] * 3 + [half_spec] * 2
    out_shape = [heads_shape] * 3 + [half_shape] * 2
    args = [x, mod, g, w]
    aliases = {}
    if kv_all is not None:
        layer_spec = pl.BlockSpec((1, None, tm, ATT_WIDTH), lambda i, j: (i, layer, j, 0))
        for arr in kv_all:
            aliases[len(args)] = len(out_specs)
            in_specs.append(pl.BlockSpec(memory_space=pl.ANY))
            out_specs.append(layer_spec)
            out_shape.append(jax.ShapeDtypeStruct(arr.shape, arr.dtype))
            args.append(arr)
    return pl.pallas_call(
        _in_proj_kernel if kv_all is None else _in_proj_kv_kernel,
        grid=(b, t // tm),
        in_specs=in_specs,
        out_specs=out_specs,
        out_shape=out_shape,
        input_output_aliases=aliases,
        compiler_params=_cparams(("parallel", "parallel"), 48),
        name="in_proj",
    )(*args)


def _head_masks(shape):
    lane = lax.broadcasted_iota(jnp.int32, shape, 1)
    return lane < HEAD_DIM, lane >= HEAD_DIM


def _scores(qh, k):
    return lax.dot_general(qh, k, (((1,), (1,)), ((), ())), preferred_element_type=F32)


def _ctx_attn_kernel(q_ref, k_ref, v_ref, o_ref):
    t = q_ref.shape[2]
    own_lanes = ((lax.broadcasted_iota(jnp.int32, (2 * t, LANES), 0) < t)
                 == (lax.broadcasted_iota(jnp.int32, (2 * t, LANES), 1) < HEAD_DIM))
    lo_lanes, _ = _head_masks((t, LANES))
    scores = []
    for hp in range(HEAD_PAIRS):
        q2 = jnp.concatenate([q_ref[0, hp], q_ref[0, hp]], axis=0)
        scores.append(_scores(jnp.where(own_lanes, q2, jnp.zeros_like(q2)), k_ref[0, hp]))
    for hp, s in enumerate(scores):
        e = jnp.exp2(s - jnp.max(s, axis=-1, keepdims=True))
        l = jnp.sum(e, axis=-1, keepdims=True)
        o2 = jnp.dot(e.astype(BF16), v_ref[0, hp], preferred_element_type=F32) / l
        o_ref[0, :, hp * LANES:(hp + 1) * LANES] = jnp.where(lo_lanes, o2[:t], o2[t:]).astype(BF16)


def _ctx_attention(q, k, v):
    b, _, t, _ = q.shape
    spec = pl.BlockSpec((1, HEAD_PAIRS, t, LANES), lambda i: (i, 0, 0, 0))
    return pl.pallas_call(
        _ctx_attn_kernel,
        grid=(b,),
        in_specs=[spec] * 3,
        out_specs=pl.BlockSpec((1, t, ATT_WIDTH), lambda i: (i, 0, 0)),
        out_shape=jax.ShapeDtypeStruct((b, t, ATT_WIDTH), BF16),
        compiler_params=_cparams(("parallel",), 32),
        name="ctx_attention",
    )(q, k, v)


def _lat_attn_kernel(q_ref, k_ref, v_ref, kc_ref, vc_ref, t2_ref, o_ref, *, rows_per_step, n_rows):
    rb = pl.program_id(2)
    kc, vc = kc_ref[0, 0], vc_ref[0, 0]
    n_loc = WIN_ROWS * GRID_W
    shape2 = (2 * GRID_W, LANES)
    own_lanes = ((lax.broadcasted_iota(jnp.int32, shape2, 0) < GRID_W)
                 == (lax.broadcasted_iota(jnp.int32, shape2, 1) < HEAD_DIM))
    lo_lanes, _ = _head_masks((GRID_W, LANES))

    def scores(i):
        r = rb * rows_per_step + i
        rs = jnp.clip(r - WIN_ROWS // 2, 0, n_rows - WIN_ROWS)
        off = rs - r + WIN_ROWS - 1
        q = q_ref[0, 0, i * GRID_W:(i + 1) * GRID_W, :]
        q2 = jnp.concatenate([q, q], axis=0)
        q2 = jnp.where(own_lanes, q2, jnp.zeros_like(q2))
        k_start = pl.multiple_of(rs * GRID_W, GRID_W)
        bias = jnp.concatenate([t2_ref[0, off + 2 * j] for j in range(WIN_ROWS // 2)], axis=1)
        s_loc = _scores(q2, k_ref[0, 0, pl.ds(k_start, n_loc), :]) + bias
        s_ctx = _scores(q2, kc)
        return jnp.concatenate([s_loc, s_ctx], axis=1), k_start

    ones_lane = lambda n: (lax.broadcasted_iota(jnp.int32, (n, LANES), 1) == 0).astype(BF16)
    ones_loc, ones_ctx = ones_lane(n_loc), ones_lane(vc.shape[0])
    vc_sum = jnp.concatenate([vc, ones_ctx], axis=1)

    def finish(i, s, m, k_start):
        e = jnp.exp2(s - m).astype(BF16)
        vb_sum = jnp.concatenate([v_ref[0, 0, pl.ds(k_start, n_loc), :], ones_loc], axis=1)
        o_sum = (jnp.dot(e[:, :n_loc], vb_sum, preferred_element_type=F32)
                 + jnp.dot(e[:, n_loc:], vc_sum, preferred_element_type=F32))
        o2 = o_sum[:, :LANES] / o_sum[:, LANES:LANES + 1]
        o_ref[0, i * GRID_W:(i + 1) * GRID_W, :] = jnp.where(lo_lanes, o2[:GRID_W], o2[GRID_W:]).astype(BF16)

    rows = [scores(0), scores(1)] + [None] * rows_per_step
    maxima = [jnp.max(rows[0][0], axis=-1, keepdims=True)] + [None] * rows_per_step
    for i in range(rows_per_step):
        if i + 2 < rows_per_step:
            rows[i + 2] = scores(i + 2)
        if i + 1 < rows_per_step:
            maxima[i + 1] = jnp.max(rows[i + 1][0], axis=-1, keepdims=True)
        finish(i, rows[i][0], maxima[i], rows[i][1])
        rows[i] = maxima[i] = None


def _lat_attention(q, k, v, kc, vc, t2):
    b, _, t, _ = q.shape
    lc = kc.shape[2]
    n_rows = t // GRID_W
    rows_per_step = min(32, n_rows)
    tq = rows_per_step * GRID_W
    full = pl.BlockSpec((1, 1, t, LANES), lambda i, j, r: (i, j, 0, 0))
    ctx = pl.BlockSpec((1, 1, lc, LANES), lambda i, j, r: (i, j, 0, 0))
    return pl.pallas_call(
        functools.partial(_lat_attn_kernel, rows_per_step=rows_per_step, n_rows=n_rows),
        grid=(b, HEAD_PAIRS, n_rows // rows_per_step),
        in_specs=[
            pl.BlockSpec((1, 1, tq, LANES), lambda i, j, r: (i, j, r, 0)),
            full, full, ctx, ctx,
            pl.BlockSpec((1, N_BIAS_TILES, LANES, LANES), lambda i, j, r: (j, 0, 0, 0)),
        ],
        out_specs=pl.BlockSpec((1, tq, LANES), lambda i, j, r: (i, r, j)),
        out_shape=jax.ShapeDtypeStruct((b, t, ATT_WIDTH), BF16),
        compiler_params=_cparams(("parallel", "parallel", "arbitrary"), 32),
        name="lat_attention",
    )(q, k, v, kc, vc, t2)


def _log_sigmoid(x):
    return jnp.minimum(x, 0.0) - jnp.log1p(jnp.exp(-jnp.abs(x)))


def _gate_matmuls(xh, wg_ref):
    half = LRU_WIDTH // 2
    xhb = xh.astype(BF16)
    return [jnp.dot(xhb[:, hf * half:(hf + 1) * half], wg_ref[hf], preferred_element_type=F32) for hf in range(2)]


def _lru_coefficients(pre, xh, bg_ref, neg_log_sig4, neg_log2_sig4):
    half = LRU_WIDTH // 2
    t_r = jnp.tanh(jnp.concatenate([pre[0][:, :half], pre[1][:, :half]], axis=1) + bg_ref[0:1, :])
    t_i = jnp.tanh(jnp.concatenate([pre[0][:, half:], pre[1][:, half:]], axis=1) + bg_ref[1:2, :])
    gate = t_r + 1.0
    a = jnp.exp2(gate * neg_log2_sig4)
    z = jnp.tanh(gate * neg_log_sig4) * (1.0 + a * a)
    root = z * lax.rsqrt(jnp.maximum(z, F32_TINY))
    return a, (root * (t_i + 1.0)) * xh


def _pipelined_recurrence(blocks, gate_inputs, xh_block, bg_ref, scales, h, recur):
    m = LRU_TIME_BLOCK * LRU_GROUP
    pre = gate_inputs(blocks[0])
    for pos, blk in enumerate(blocks):
        following = gate_inputs(blocks[pos + 1]) if pos + 1 < len(blocks) else None
        a, b = _lru_coefficients(pre, xh_block(slice(blk * m, (blk + 1) * m)), bg_ref, *scales)
        h = recur(blk, a, b, h)
        pre = following
    return h


def _log_sigmoid_scales(la_ref):
    neg_log_sig4 = (-0.5 * LRU_C) * _log_sigmoid(la_ref[...])
    return neg_log_sig4, neg_log_sig4 * (-LOG2_E)


def _lru_bwd_kernel(xb_ref, xprev_ref, xnext_ref, cw_ref, cb_ref, wg_ref, bg_ref, la_ref, h0_ref,
                    xh_ref, hb_ref, hfin_ref, x_s, carry_s, *, tc):
    j = pl.program_id(1)
    n = pl.num_programs(1)
    cj = n - 1 - j
    x_pitch = tc + X_PAD
    n_slabs = LRU_WIDTH // LANES
    lane_slab = lambda v, c: v[:, c * LANES:(c + 1) * LANES]

    @pl.when(j == 0)
    def _():
        carry_s[...] = h0_ref[0]

    def park(bi, carry):
        base = pl.multiple_of(bi * x_pitch, SUBLANES)
        pieces = ((0, jnp.where(cj > 0, xprev_ref[bi], 0.0)), (SUBLANES, xb_ref[bi]),
                  (SUBLANES + tc, jnp.where(cj < n - 1, xnext_ref[bi], 0.0)))
        for start, v in pieces:
            for c in range(n_slabs):
                x_s[c, pl.ds(base + start, v.shape[0]), :] = lane_slab(v, c)
        return carry

    lax.fori_loop(0, LRU_GROUP, park, 0)

    scales = _log_sigmoid_scales(la_ref)

    def time_rows(t):
        rows = pl.ds(SUBLANES + t, LRU_GROUP, stride=x_pitch)
        return jnp.concatenate([x_s[c, rows, :] for c in range(n_slabs)], axis=1)

    m = LRU_TIME_BLOCK * LRU_GROUP

    def conv_block(blk):
        t0 = blk * LRU_TIME_BLOCK
        xw = jnp.concatenate([time_rows(t0 + k) for k in range(-CONV_LEFT, LRU_TIME_BLOCK + CONV_W - 1 - CONV_LEFT)],
                             axis=0)
        xh = cb_ref[...]
        for tap in range(CONV_W):
            xh = xh + cw_ref[tap:tap + 1, :] * xw[tap * LRU_GROUP:tap * LRU_GROUP + m]
        xh_ref[0, blk * m:(blk + 1) * m, :] = xh
        return _gate_matmuls(xh, wg_ref)

    def recur(blk, a, b, h):
        for i in reversed(range(LRU_TIME_BLOCK)):
            rows = slice(i * LRU_GROUP, (i + 1) * LRU_GROUP)
            h = a[rows] * h + b[rows]
            hb_ref[0, blk * m + i * LRU_GROUP:blk * m + (i + 1) * LRU_GROUP, :] = h
        return h

    blocks = list(reversed(range(tc // LRU_TIME_BLOCK)))
    carry_s[...] = _pipelined_recurrence(blocks, conv_block, lambda rows: xh_ref[0, rows, :], bg_ref, scales,
                                         carry_s[...], recur)

    @pl.when(j == n - 1)
    def _():
        hfin_ref[0] = carry_s[...]


def _lru_fwd_kernel(xh_ref, hb_ref, gy_ref, wg_ref, bg_ref, la_ref, h0_ref, out_ref, hfin_ref,
                    s_s, carry_s, *, tc):
    j = pl.program_id(1)
    n = pl.num_programs(1)
    pitch = tc + SCAN_PAD
    n_slabs = LRU_WIDTH // LANES
    m = LRU_TIME_BLOCK * LRU_GROUP

    @pl.when(j == 0)
    def _():
        carry_s[...] = h0_ref[0]

    scales = _log_sigmoid_scales(la_ref)

    def recur(blk, a, b, h):
        for i in range(LRU_TIME_BLOCK):
            rows = slice(i * LRU_GROUP, (i + 1) * LRU_GROUP)
            h = a[rows] * h + b[rows]
            both = h + hb_ref[0, blk * m + i * LRU_GROUP:blk * m + (i + 1) * LRU_GROUP, :]
            for c in range(n_slabs):
                s_s[c, pl.ds(blk * LRU_TIME_BLOCK + i, LRU_GROUP, stride=pitch), :] = both[:, c * LANES:(c + 1) * LANES]
        return h

    xh_block = lambda rows: xh_ref[0, rows, :]
    carry_s[...] = _pipelined_recurrence(
        list(range(tc // LRU_TIME_BLOCK)),
        lambda blk: _gate_matmuls(xh_block(slice(blk * m, (blk + 1) * m)), wg_ref),
        xh_block, bg_ref, scales, carry_s[...], recur)

    def emit(bi, carry):
        rows = pl.ds(pl.multiple_of(bi * pitch, SUBLANES), tc)
        both = jnp.concatenate([s_s[c, rows, :] for c in range(n_slabs)], axis=1)
        out_ref[bi] = (both * gy_ref[bi]).astype(BF16)
        return carry

    lax.fori_loop(0, LRU_GROUP, emit, 0)

    @pl.when(j == n - 1)
    def _():
        hfin_ref[0] = carry_s[...]


def _lru(xb, gy, conv_w, conv_b, wg, bg, la, h0):
    b, t, c = xb.shape
    assert b % LRU_GROUP == 0
    groups = b // LRU_GROUP
    tc = min(256, t)
    n = t // tc
    per = tc // SUBLANES
    assert tc % (2 * SUBLANES) == 0 and tc % LRU_TIME_BLOCK == 0
    h0 = h0.reshape(groups, LRU_GROUP, 2, c)
    state = pl.BlockSpec((1, LRU_GROUP, c), lambda g, j: (g, 0, 0))
    state_shape = jax.ShapeDtypeStruct((groups, LRU_GROUP, c), F32)
    tm_rows = tc * LRU_GROUP
    tm_shape = jax.ShapeDtypeStruct((groups, t * LRU_GROUP, c), F32)
    slabs = lambda pad: pltpu.VMEM((c // LANES, LRU_GROUP * (tc + pad), LANES), F32)
    carry = pltpu.VMEM((LRU_GROUP, c), F32)
    gate_specs = [_resident((2, c // 2, c)), _resident((2, c)), _resident((1, c)), state]

    rev = lambda j: n - 1 - j
    rev_tm = pl.BlockSpec((1, tm_rows, c), lambda g, j: (g, rev(j), 0))
    halo = lambda f: pl.BlockSpec((LRU_GROUP, SUBLANES, c), lambda g, j: (g, f(rev(j)), 0))
    xh, hb, hb_fin = pl.pallas_call(
        functools.partial(_lru_bwd_kernel, tc=tc),
        grid=(groups, n),
        in_specs=[
            pl.BlockSpec((LRU_GROUP, tc, c), lambda g, j: (g, rev(j), 0)),
            halo(lambda cj: jnp.maximum(cj * per - 1, 0)),
            halo(lambda cj: jnp.minimum((cj + 1) * per, t // SUBLANES - 1)),
            _resident((CONV_W, c)),
            _resident((1, c)),
        ] + gate_specs,
        out_specs=[rev_tm, rev_tm, state],
        out_shape=[tm_shape, tm_shape, state_shape],
        scratch_shapes=[slabs(X_PAD), carry],
        compiler_params=_cparams(("parallel", "arbitrary"), 56),
        name="lru_bwd",
    )(xb, xb, xb, conv_w, conv_b, wg[1], bg[1], la[1:2], h0[:, :, 1])

    fwd_tm = pl.BlockSpec((1, tm_rows, c), lambda g, j: (g, j, 0))
    chunk = pl.BlockSpec((LRU_GROUP, tc, c), lambda g, j: (g, j, 0))
    rec, hf_fin = pl.pallas_call(
        functools.partial(_lru_fwd_kernel, tc=tc),
        grid=(groups, n),
        in_specs=[fwd_tm, fwd_tm, chunk] + gate_specs,
        out_specs=[chunk, state],
        out_shape=[jax.ShapeDtypeStruct((b, t, c), BF16), state_shape],
        scratch_shapes=[slabs(SCAN_PAD), carry],
        compiler_params=_cparams(("parallel", "arbitrary"), 56),
        name="lru_fwd",
    )(xh, hb, gy, wg[0], bg[0], la[0:1], h0[:, :, 0])
    return rec, jnp.stack([hf_fin.reshape(b, c), hb_fin.reshape(b, c)], axis=1)


def _out_ffn_kernel(x_ref, att_ref, rec_ref, mod_ref, g_ref, wo_ref, wg_ref, wu_ref, wd_ref, *rest, final_norm):
    o_ref = rest[-1]
    mix = jnp.concatenate([att_ref[0], rec_ref[0]], axis=1)
    x1 = x_ref[0] + mod_ref[0, 2:3, :] * jnp.dot(mix, wo_ref[...], preferred_element_type=F32)
    h2 = (_rmsnorm(x1, g_ref[...]) * (1.0 + mod_ref[0, 4:5, :]) + mod_ref[0, 3:4, :]).astype(BF16)
    gate = jnp.dot(h2, wg_ref[...], preferred_element_type=F32)
    up = jnp.dot(h2, wu_ref[...], preferred_element_type=F32)
    act = ((gate * jax.nn.sigmoid(gate)) * up).astype(BF16)
    x2 = x1 + mod_ref[0, 5:6, :] * jnp.dot(act, wd_ref[...], preferred_element_type=F32)
    if final_norm:
        x2 = _rmsnorm(x2, rest[0][...])
    o_ref[0] = x2


def _out_ffn(x, att, rec, mod, g2, w_out, w_gate, w_up, w_down, layer, g_final=None):
    b, t, d = x.shape
    d_ff = w_gate.shape[-1]
    tm = min(512, t)
    shared = mod.shape[0] == 1
    mod_map = (lambda i, j: (0, 0, 0)) if shared else (lambda i, j: (i, 0, 0))
    tok = lambda w: pl.BlockSpec((1, tm, w), lambda i, j: (i, j, 0))
    in_specs = [
        tok(d), tok(ATT_WIDTH), tok(LRU_WIDTH),
        pl.BlockSpec((1, 6, d), mod_map),
        _resident((1, d)),
        _layer_resident((d, d), layer),
        _layer_resident((d, d_ff), layer),
        _layer_resident((d, d_ff), layer),
        _layer_resident((d_ff, d), layer),
    ]
    args = [x, att, rec, mod, g2, w_out, w_gate, w_up, w_down]
    if g_final is not None:
        in_specs.append(_resident((1, d)))
        args.append(g_final)
    return pl.pallas_call(
        functools.partial(_out_ffn_kernel, final_norm=g_final is not None),
        grid=(b, t // tm),
        in_specs=in_specs,
        out_specs=tok(d),
        out_shape=jax.ShapeDtypeStruct((b, t, d), F32),
        compiler_params=_cparams(("parallel", "parallel"), 56),
        name="out_ffn",
    )(*args)


def _block_diag_halves(w):
    per_half = LRU_BLOCKS // 2
    w4 = w.reshape(2, per_half, LRU_BLOCK, LRU_BLOCK)
    eye = jnp.eye(per_half, dtype=w.dtype)
    return jnp.einsum('hnjk,nm->hnjmk', w4, eye).reshape(2, per_half * LRU_BLOCK, per_half * LRU_BLOCK)


def _head_pair_layout(cache):
    b, depth, p = cache.shape[:3]
    return cache.reshape(b, depth, p, HEAD_PAIRS, LANES).transpose(1, 0, 3, 2, 4).astype(BF16)


def kernel(x_prompt, x_sample, cache_k, cache_v, state_lru, c, c_ctx, w_mod, b_mod, norm1, norm2, w_in, w_out, rpb, conv_w, conv_b, lru_a, lru_wr, lru_br, lru_wi, lru_bi, w_gate, w_up, w_down, norm_final):
    depth = w_in.shape[0]
    n_ctx, n_lat = x_prompt.shape[0], x_sample.shape[0]
    d = x_prompt.shape[-1]
    assert n_lat + 1 <= MOD_ROWS

    cvec = jnp.zeros((MOD_ROWS, d), F32).at[:n_lat].set(c).at[n_lat].set(c_ctx)
    mod = _modulation(cvec, w_mod, b_mod).reshape(depth, MOD_ROWS, 6, d)
    t2 = _bias_tables(rpb)
    ck, cv = _head_pair_layout(cache_k), _head_pair_layout(cache_v)

    w_in_b, w_out_b = w_in.astype(BF16), w_out.astype(BF16)
    w_gate_b, w_up_b, w_down_b = w_gate.astype(BF16), w_up.astype(BF16), w_down.astype(BF16)
    wg = jnp.stack([
        jnp.stack([jnp.concatenate([_block_diag_halves(lru_wr[l, dr]), _block_diag_halves(lru_wi[l, dr])], axis=-1)
                   for dr in range(2)]) for l in range(depth)]).astype(BF16)
    bg = 0.5 * jnp.stack([lru_br, lru_bi], axis=2)
    conv_w_half, conv_b_half = 0.5 * conv_w, 0.5 * conv_b
    h0_ctx = jnp.zeros((n_ctx, 2, LRU_WIDTH), F32)

    def layer(x, l, mod_l, attend, h0, last, kv_all=None):
        outs = _in_proj(x, mod_l, norm1[l][None], w_in_b, l, kv_all)
        q, k, v, xb, gy = outs[:5]
        att = attend(q, k, v)
        rec, h_fin = _lru(xb, gy, conv_w_half[l], conv_b_half[l][None], wg[l], bg[l], lru_a[l], h0)
        x = _out_ffn(x, att, rec, mod_l, norm2[l][None], w_out_b, w_gate_b, w_up_b, w_down_b, l,
                     g_final=norm_final[None] if last else None)
        return x, outs[5:], h_fin

    xp, xs = x_prompt, x_sample
    kv_all = [jnp.zeros((n_ctx, depth, x_prompt.shape[1], ATT_WIDTH), F32) for _ in range(2)]
    hs_out = []
    for l in range(depth):
        last = l == depth - 1
        xp, kv_all, h_l = layer(xp, l, mod[l, n_lat:n_lat + 1], _ctx_attention, h0_ctx, last, kv_all)
        hs_out.append(h_l)
        attend_lat = functools.partial(_lat_attention, kc=ck[l], vc=cv[l], t2=t2[l])
        xs, _, _ = layer(xs, l, mod[l, :n_lat], attend_lat, state_lru[:, l], last)
    new_k, new_v = (a.reshape(n_ctx, depth, -1, N_ATT_HEADS, HEAD_DIM) for a in kv_all)
    return (xp, xs, new_k, new_v, jnp.stack(hs_out, axis=1))
```

```python
import functools

import jax
import jax.numpy as jnp
from jax import lax
from jax.experimental import pallas as pl
from jax.experimental.pallas import tpu as pltpu

F32 = jnp.float32
BF16 = jnp.bfloat16

D_MODEL = 1024
GRID_W = 64
N_ATT_HEADS = 8
HEAD_DIM = 64
ATT_WIDTH = N_ATT_HEADS * HEAD_DIM
ATT_SCALE = HEAD_DIM ** -0.5
WIN_ROWS = 8
WIN_COLS = 16
LRU_WIDTH = D_MODEL // 2
LRU_BLOCKS = 8
LRU_BLOCK = LRU_WIDTH // LRU_BLOCKS
CONV_W = 4
CONV_LEFT = 2
LRU_C = 8.0
IN_COLS = 3 * ATT_WIDTH + 2 * LRU_WIDTH
IN_PROJ_BLOCKS = {'q': (0, ATT_WIDTH), 'k': (ATT_WIDTH, ATT_WIDTH), 'v': (2 * ATT_WIDTH, ATT_WIDTH),
                  'xb': (3 * ATT_WIDTH, LRU_WIDTH), 'yb': (3 * ATT_WIDTH + LRU_WIDTH, LRU_WIDTH)}
IN_PROJ_ORDER = ('yb', 'q', 'k', 'v', 'xb')
EPS = 1e-6
NEG_INF = -1e30
F32_TINY = float(jnp.finfo(jnp.float32).tiny)
LOG2_E = 1.4426950408889634
Q_SCALE = ATT_SCALE * LOG2_E

LANES = 128
SUBLANES = 8
HEAD_PAIRS = ATT_WIDTH // LANES
N_ROW_IDX = 2 * WIN_ROWS - 1
N_COL_IDX = 2 * WIN_COLS - 1
N_BIAS_TILES = N_ROW_IDX - 1
MOD_ROWS = 16
LRU_GROUP = SUBLANES
SCAN_PAD = SUBLANES
X_PAD = 3 * SUBLANES
LRU_TIME_BLOCK = 32
MIB = 1024 * 1024


def _cparams(semantics, vmem_mib):
    return pltpu.CompilerParams(dimension_semantics=semantics, vmem_limit_bytes=vmem_mib * MIB)


def _resident(shape):
    zeros = (0,) * len(shape)
    return pl.BlockSpec(shape, lambda *_: zeros, pipeline_mode=pl.Buffered(1))


def _layer_resident(shape, layer):
    index = (layer,) + (0,) * len(shape)
    return pl.BlockSpec((None,) + tuple(shape), lambda *_: index, pipeline_mode=pl.Buffered(1))


def _in_proj_columns():
    out, lo = [], 0
    for name in IN_PROJ_ORDER:
        width = IN_PROJ_BLOCKS[name][1]
        out.append((lo, width))
        lo += width
    return out


def _rmsnorm(x, g):
    return (x * lax.rsqrt(jnp.mean(x * x, axis=-1, keepdims=True) + EPS)) * g


def _mod_kernel(c_ref, w_ref, b_ref, o_ref):
    cv = c_ref[...]
    s = (cv * jax.nn.sigmoid(cv)).astype(BF16)
    o_ref[0] = jnp.dot(s, w_ref[0].astype(BF16), preferred_element_type=F32) + b_ref[0]


def _modulation(cvec, w_mod, b_mod):
    depth, d, n = w_mod.shape
    tn = n // 4
    return pl.pallas_call(
        _mod_kernel,
        grid=(depth, n // tn),
        in_specs=[
            pl.BlockSpec((MOD_ROWS, d), lambda l, j: (0, 0)),
            pl.BlockSpec((1, d, tn), lambda l, j: (l, 0, j)),
            pl.BlockSpec((1, 1, tn), lambda l, j: (l, 0, j)),
        ],
        out_specs=pl.BlockSpec((1, MOD_ROWS, tn), lambda l, j: (l, 0, j)),
        out_shape=jax.ShapeDtypeStruct((depth, MOD_ROWS, n), F32),
        compiler_params=_cparams(("arbitrary", "arbitrary"), 40),
        name="adaln_mod",
    )(cvec, w_mod, b_mod.reshape(depth, 1, n))


def _bias_table_kernel(rpb_ref, o_ref):
    shape = (GRID_W, LANES)
    c = lax.broadcasted_iota(jnp.int32, shape, 0)
    lane = lax.broadcasted_iota(jnp.int32, shape, 1)
    upper = lane >= GRID_W
    kc = jnp.where(upper, lane - GRID_W, lane)
    cs = jnp.clip(c - WIN_COLS // 2, 0, GRID_W - WIN_COLS)
    ok = (kc >= cs) & (kc < cs + WIN_COLS)
    lo_shift = LANES - (WIN_COLS - 1)
    hi_shift = (lo_shift + GRID_W) % LANES
    for h in range(2):
        for i in range(N_BIAS_TILES):
            lo = pltpu.roll(jnp.broadcast_to(rpb_ref[0, h, i:i + 1, :], shape), lo_shift, 1, stride=1, stride_axis=0)
            hi = pltpu.roll(jnp.broadcast_to(rpb_ref[0, h, i + 1:i + 2, :], shape), hi_shift, 1, stride=1, stride_axis=0)
            o_ref[0, i, h * GRID_W:(h + 1) * GRID_W, :] = jnp.where(ok, jnp.where(upper, hi, lo) * LOG2_E, NEG_INF)


def _bias_tables(rpb):
    depth = rpb.shape[0]
    n = depth * HEAD_PAIRS
    padded = jnp.pad(rpb, ((0, 0), (0, 0), (0, 2 * SUBLANES - N_ROW_IDX), (0, LANES - N_COL_IDX)))
    out = pl.pallas_call(
        _bias_table_kernel,
        grid=(n,),
        in_specs=[pl.BlockSpec((1, 2, 2 * SUBLANES, LANES), lambda i: (i, 0, 0, 0))],
        out_specs=pl.BlockSpec((1, N_BIAS_TILES, LANES, LANES), lambda i: (i, 0, 0, 0)),
        out_shape=jax.ShapeDtypeStruct((n, N_BIAS_TILES, LANES, LANES), F32),
        compiler_params=_cparams(("arbitrary",), 16),
        name="bias_tables",
    )(padded.reshape(n, 2, 2 * SUBLANES, LANES))
    return out.reshape(depth, HEAD_PAIRS, N_BIAS_TILES, LANES, LANES)


def _in_proj_kernel(x_ref, mod_ref, g_ref, w_ref, q_ref, k_ref, v_ref, xb_ref, gy_ref, *kv_f32_refs):
    h = _rmsnorm(x_ref[0], g_ref[...]) * (1.0 + mod_ref[0, 1:2, :]) + mod_ref[0, 0:1, :]
    p = jnp.dot(h.astype(BF16), w_ref[...], preferred_element_type=F32)
    gate, q, k, v, xb = (p[:, lo:lo + width] for lo, width in _in_proj_columns())
    gy_ref[0] = jax.nn.gelu(gate)
    for j in range(HEAD_PAIRS):
        lanes = slice(j * LANES, (j + 1) * LANES)
        q_ref[0, j] = (q[:, lanes] * Q_SCALE).astype(BF16)
        k_ref[0, j] = k[:, lanes].astype(BF16)
        v_ref[0, j] = v[:, lanes].astype(BF16)
    if kv_f32_refs:
        kf_ref, vf_ref = kv_f32_refs
        kf_ref[0] = k
        vf_ref[0] = v
    xb_ref[0] = xb


def _in_proj_kv_kernel(x_ref, mod_ref, g_ref, w_ref, k_all_ref, v_all_ref, *out_refs):
    del k_all_ref, v_all_ref
    _in_proj_kernel(x_ref, mod_ref, g_ref, w_ref, *out_refs)


def _in_proj(x, mod, g, w, layer, kv_all=None):
    b, t, d = x.shape
    tm = min(1024, t)
    shared = mod.shape[0] == 1
    mod_map = (lambda i, j: (0, 0, 0)) if shared else (lambda i, j: (i, 0, 0))
    heads_spec = pl.BlockSpec((1, HEAD_PAIRS, tm, LANES), lambda i, j: (i, 0, j, 0))
    heads_shape = jax.ShapeDtypeStruct((b, HEAD_PAIRS, t, LANES), BF16)
    half_spec = pl.BlockSpec((1, tm, LRU_WIDTH), lambda i, j: (i, j, 0))
    half_shape = jax.ShapeDtypeStruct((b, t, LRU_WIDTH), F32)
    in_specs = [
        pl.BlockSpec((1, tm, d), lambda i, j: (i, j, 0)),
        pl.BlockSpec((1, 6, d), mod_map),
        _resident((1, d)),
        _layer_resident((d, IN_COLS), layer),
    ]
    out_specs = [heads_spec] * 3 + [half_spec] * 2
    out_shape = [heads_shape] * 3 + [half_shape] * 2
    args = [x, mod, g, w]
    aliases = {}
    if kv_all is not None:
        layer_spec = pl.BlockSpec((1, None, tm, ATT_WIDTH), lambda i, j: (i, layer, j, 0))
        for arr in kv_all:
            aliases[len(args)] = len(out_specs)
            in_specs.append(pl.BlockSpec(memory_space=pl.ANY))
            out_specs.append(layer_spec)
            out_shape.append(jax.ShapeDtypeStruct(arr.shape, arr.dtype))
            args.append(arr)
    return pl.pallas_call(
        _in_proj_kernel if kv_all is None else _in_proj_kv_kernel,
        grid=(b, t // tm),
        in_specs=in_specs,
        out_specs=out_specs,
        out_shape=out_shape,
        input_output_aliases=aliases,
        compiler_params=_cparams(("parallel", "parallel"), 48),
        name="in_proj",
    )(*args)


def _head_masks(shape):
    lane = lax.broadcasted_iota(jnp.int32, shape, 1)
    return lane < HEAD_DIM, lane >= HEAD_DIM


def _scores(qh, k):
    return lax.dot_general(qh, k, (((1,), (1,)), ((), ())), preferred_element_type=F32)


def _ctx_attn_kernel(q_ref, k_ref, v_ref, o_ref):
    t = q_ref.shape[2]
    own_lanes = ((lax.broadcasted_iota(jnp.int32, (2 * t, LANES), 0) < t)
                 == (lax.broadcasted_iota(jnp.int32, (2 * t, LANES), 1) < HEAD_DIM))
    lo_lanes, _ = _head_masks((t, LANES))
    scores = []
    for hp in range(HEAD_PAIRS):
        q2 = jnp.concatenate([q_ref[0, hp], q_ref[0, hp]], axis=0)
        scores.append(_scores(jnp.where(own_lanes, q2, jnp.zeros_like(q2)), k_ref[0, hp]))
    for hp, s in enumerate(scores):
        e = jnp.exp2(s - jnp.max(s, axis=-1, keepdims=True))
        l = jnp.sum(e, axis=-1, keepdims=True)
        o2 = jnp.dot(e.astype(BF16), v_ref[0, hp], preferred_element_type=F32) / l
        o_ref[0, :, hp * LANES:(hp + 1) * LANES] = jnp.where(lo_lanes, o2[:t], o2[t:]).astype(BF16)


def _ctx_attention(q, k, v):
    b, _, t, _ = q.shape
    spec = pl.BlockSpec((1, HEAD_PAIRS, t, LANES), lambda i: (i, 0, 0, 0))
    return pl.pallas_call(
        _ctx_attn_kernel,
        grid=(b,),
        in_specs=[spec] * 3,
        out_specs=pl.BlockSpec((1, t, ATT_WIDTH), lambda i: (i, 0, 0)),
        out_shape=jax.ShapeDtypeStruct((b, t, ATT_WIDTH), BF16),
        compiler_params=_cparams(("parallel",), 32),
        name="ctx_attention",
    )(q, k, v)


def _lat_attn_kernel(q_ref, k_ref, v_ref, kc_ref, vc_ref, t2_ref, o_ref, *, rows_per_step, n_rows):
    rb = pl.program_id(2)
    kc, vc = kc_ref[0, 0], vc_ref[0, 0]
    n_loc = WIN_ROWS * GRID_W
    shape2 = (2 * GRID_W, LANES)
    own_lanes = ((lax.broadcasted_iota(jnp.int32, shape2, 0) < GRID_W)
                 == (lax.broadcasted_iota(jnp.int32, shape2, 1) < HEAD_DIM))
    lo_lanes, _ = _head_masks((GRID_W, LANES))

    def scores(i):
        r = rb * rows_per_step + i
        rs = jnp.clip(r - WIN_ROWS // 2, 0, n_rows - WIN_ROWS)
        off = rs - r + WIN_ROWS - 1
        q = q_ref[0, 0, i * GRID_W:(i + 1) * GRID_W, :]
        q2 = jnp.concatenate([q, q], axis=0)
        q2 = jnp.where(own_lanes, q2, jnp.zeros_like(q2))
        k_start = pl.multiple_of(rs * GRID_W, GRID_W)
        bias = jnp.concatenate([t2_ref[0, off + 2 * j] for j in range(WIN_ROWS // 2)], axis=1)
        s_loc = _scores(q2, k_ref[0, 0, pl.ds(k_start, n_loc), :]) + bias
        s_ctx = _scores(q2, kc)
        return jnp.concatenate([s_loc, s_ctx], axis=1), k_start

    ones_lane = lambda n: (lax.broadcasted_iota(jnp.int32, (n, LANES), 1) == 0).astype(BF16)
    ones_loc, ones_ctx = ones_lane(n_loc), ones_lane(vc.shape[0])
    vc_sum = jnp.concatenate([vc, ones_ctx], axis=1)

    def finish(i, s, m, k_start):
        e = jnp.exp2(s - m).astype(BF16)
        vb_sum = jnp.concatenate([v_ref[0, 0, pl.ds(k_start, n_loc), :], ones_loc], axis=1)
        o_sum = (jnp.dot(e[:, :n_loc], vb_sum, preferred_element_type=F32)
                 + jnp.dot(e[:, n_loc:], vc_sum, preferred_element_type=F32))
        o2 = o_sum[:, :LANES] / o_sum[:, LANES:LANES + 1]
        o_ref[0, i * GRID_W:(i + 1) * GRID_W, :] = jnp.where(lo_lanes, o2[:GRID_W], o2[GRID_W:]).astype(BF16)

    rows = [scores(0), scores(1)] + [None] * rows_per_step
    maxima = [jnp.max(rows[0][0], axis=-1, keepdims=True)] + [None] * rows_per_step
    for i in range(rows_per_step):
        if i + 2 < rows_per_step:
            rows[i + 2] = scores(i + 2)
        if i + 1 < rows_per_step:
            maxima[i + 1] = jnp.max(rows[i + 1][0], axis=-1, keepdims=True)
        finish(i, rows[i][0], maxima[i], rows[i][1])
        rows[i] = maxima[i] = None


def _lat_attention(q, k, v, kc, vc, t2):
    b, _, t, _ = q.shape
    lc = kc.shape[2]
    n_rows = t // GRID_W
    rows_per_step = min(32, n_rows)
    tq = rows_per_step * GRID_W
    full = pl.BlockSpec((1, 1, t, LANES), lambda i, j, r: (i, j, 0, 0))
    ctx = pl.BlockSpec((1, 1, lc, LANES), lambda i, j, r: (i, j, 0, 0))
    return pl.pallas_call(
        functools.partial(_lat_attn_kernel, rows_per_step=rows_per_step, n_rows=n_rows),
        grid=(b, HEAD_PAIRS, n_rows // rows_per_step),
        in_specs=[
            pl.BlockSpec((1, 1, tq, LANES), lambda i, j, r: (i, j, r, 0)),
            full, full, ctx, ctx,
            pl.BlockSpec((1, N_BIAS_TILES, LANES, LANES), lambda i, j, r: (j, 0, 0, 0)),
        ],
        out_specs=pl.BlockSpec((1, tq, LANES), lambda i, j, r: (i, r, j)),
        out_shape=jax.ShapeDtypeStruct((b, t, ATT_WIDTH), BF16),
        compiler_params=_cparams(("parallel", "parallel", "arbitrary"), 32),
        name="lat_attention",
    )(q, k, v, kc, vc, t2)


def _log_sigmoid(x):
    return jnp.minimum(x, 0.0) - jnp.log1p(jnp.exp(-jnp.abs(x)))


def _gate_matmuls(xh, wg_ref):
    half = LRU_WIDTH // 2
    xhb = xh.astype(BF16)
    return [jnp.dot(xhb[:, hf * half:(hf + 1) * half], wg_ref[hf], preferred_element_type=F32) for hf in range(2)]


def _lru_coefficients(pre, xh, bg_ref, neg_log_sig4, neg_log2_sig4):
    half = LRU_WIDTH // 2
    t_r = jnp.tanh(jnp.concatenate([pre[0][:, :half], pre[1][:, :half]], axis=1) + bg_ref[0:1, :])
    t_i = jnp.tanh(jnp.concatenate([pre[0][:, half:], pre[1][:, half:]], axis=1) + bg_ref[1:2, :])
    gate = t_r + 1.0
    a = jnp.exp2(gate * neg_log2_sig4)
    z = jnp.tanh(gate * neg_log_sig4) * (1.0 + a * a)
    root = z * lax.rsqrt(jnp.maximum(z, F32_TINY))
    return a, (root * (t_i + 1.0)) * xh


def _pipelined_recurrence(blocks, gate_inputs, xh_block, bg_ref, scales, h, recur):
    m = LRU_TIME_BLOCK * LRU_GROUP
    pre = gate_inputs(blocks[0])
    for pos, blk in enumerate(blocks):
        following = gate_inputs(blocks[pos + 1]) if pos + 1 < len(blocks) else None
        a, b = _lru_coefficients(pre, xh_block(slice(blk * m, (blk + 1) * m)), bg_ref, *scales)
        h = recur(blk, a, b, h)
        pre = following
    return h


def _log_sigmoid_scales(la_ref):
    neg_log_sig4 = (-0.5 * LRU_C) * _log_sigmoid(la_ref[...])
    return neg_log_sig4, neg_log_sig4 * (-LOG2_E)


def _lru_bwd_kernel(xb_ref, xprev_ref, xnext_ref, cw_ref, cb_ref, wg_ref, bg_ref, la_ref, h0_ref,
                    xh_ref, hb_ref, hfin_ref, x_s, carry_s, *, tc):
    j = pl.program_id(1)
    n = pl.num_programs(1)
    cj = n - 1 - j
    x_pitch = tc + X_PAD
    n_slabs = LRU_WIDTH // LANES
    lane_slab = lambda v, c: v[:, c * LANES:(c + 1) * LANES]

    @pl.when(j == 0)
    def _():
        carry_s[...] = h0_ref[0]

    def park(bi, carry):
        base = pl.multiple_of(bi * x_pitch, SUBLANES)
        pieces = ((0, jnp.where(cj > 0, xprev_ref[bi], 0.0)), (SUBLANES, xb_ref[bi]),
                  (SUBLANES + tc, jnp.where(cj < n - 1, xnext_ref[bi], 0.0)))
        for start, v in pieces:
            for c in range(n_slabs):
                x_s[c, pl.ds(base + start, v.shape[0]), :] = lane_slab(v, c)
        return carry

    lax.fori_loop(0, LRU_GROUP, park, 0)

    scales = _log_sigmoid_scales(la_ref)

    def time_rows(t):
        rows = pl.ds(SUBLANES + t, LRU_GROUP, stride=x_pitch)
        return jnp.concatenate([x_s[c, rows, :] for c in range(n_slabs)], axis=1)

    m = LRU_TIME_BLOCK * LRU_GROUP

    def conv_block(blk):
        t0 = blk * LRU_TIME_BLOCK
        xw = jnp.concatenate([time_rows(t0 + k) for k in range(-CONV_LEFT, LRU_TIME_BLOCK + CONV_W - 1 - CONV_LEFT)],
                             axis=0)
        xh = cb_ref[...]
        for tap in range(CONV_W):
            xh = xh + cw_ref[tap:tap + 1, :] * xw[tap * LRU_GROUP:tap * LRU_GROUP + m]
        xh_ref[0, blk * m:(blk + 1) * m, :] = xh
        return _gate_matmuls(xh, wg_ref)

    def recur(blk, a, b, h):
        for i in reversed(range(LRU_TIME_BLOCK)):
            rows = slice(i * LRU_GROUP, (i + 1) * LRU_GROUP)
            h = a[rows] * h + b[rows]
            hb_ref[0, blk * m + i * LRU_GROUP:blk * m + (i + 1) * LRU_GROUP, :] = h
        return h

    blocks = list(reversed(range(tc // LRU_TIME_BLOCK)))
    carry_s[...] = _pipelined_recurrence(blocks, conv_block, lambda rows: xh_ref[0, rows, :], bg_ref, scales,
                                         carry_s[...], recur)

    @pl.when(j == n - 1)
    def _():
        hfin_ref[0] = carry_s[...]


def _lru_fwd_kernel(xh_ref, hb_ref, gy_ref, wg_ref, bg_ref, la_ref, h0_ref, out_ref, hfin_ref,
                    s_s, carry_s, *, tc):
    j = pl.program_id(1)
    n = pl.num_programs(1)
    pitch = tc + SCAN_PAD
    n_slabs = LRU_WIDTH // LANES
    m = LRU_TIME_BLOCK * LRU_GROUP

    @pl.when(j == 0)
    def _():
        carry_s[...] = h0_ref[0]

    scales = _log_sigmoid_scales(la_ref)

    def recur(blk, a, b, h):
        for i in range(LRU_TIME_BLOCK):
            rows = slice(i * LRU_GROUP, (i + 1) * LRU_GROUP)
            h = a[rows] * h + b[rows]
            both = h + hb_ref[0, blk * m + i * LRU_GROUP:blk * m + (i + 1) * LRU_GROUP, :]
            for c in range(n_slabs):
                s_s[c, pl.ds(blk * LRU_TIME_BLOCK + i, LRU_GROUP, stride=pitch), :] = both[:, c * LANES:(c + 1) * LANES]
        return h

    xh_block = lambda rows: xh_ref[0, rows, :]
    carry_s[...] = _pipelined_recurrence(
        list(range(tc // LRU_TIME_BLOCK)),
        lambda blk: _gate_matmuls(xh_block(slice(blk * m, (blk + 1) * m)), wg_ref),
        xh_block, bg_ref, scales, carry_s[...], recur)

    def emit(bi, carry):
        rows = pl.ds(pl.multiple_of(bi * pitch, SUBLANES), tc)
        both = jnp.concatenate([s_s[c, rows, :] for c in range(n_slabs)], axis=1)
        out_ref[bi] = (both * gy_ref[bi]).astype(BF16)
        return carry

    lax.fori_loop(0, LRU_GROUP, emit, 0)

    @pl.when(j == n - 1)
    def _():
        hfin_ref[0] = carry_s[...]


def _lru(xb, gy, conv_w, conv_b, wg, bg, la, h0):
    b, t, c = xb.shape
    assert b % LRU_GROUP == 0
    groups = b // LRU_GROUP
    tc = min(256, t)
    n = t // tc
    per = tc // SUBLANES
    assert tc % (2 * SUBLANES) == 0 and tc % LRU_TIME_BLOCK == 0
    h0 = h0.reshape(groups, LRU_GROUP, 2, c)
    state = pl.BlockSpec((1, LRU_GROUP, c), lambda g, j: (g, 0, 0))
    state_shape = jax.ShapeDtypeStruct((groups, LRU_GROUP, c), F32)
    tm_rows = tc * LRU_GROUP
    tm_shape = jax.ShapeDtypeStruct((groups, t * LRU_GROUP, c), F32)
    slabs = lambda pad: pltpu.VMEM((c // LANES, LRU_GROUP * (tc + pad), LANES), F32)
    carry = pltpu.VMEM((LRU_GROUP, c), F32)
    gate_specs = [_resident((2, c // 2, c)), _resident((2, c)), _resident((1, c)), state]

    rev = lambda j: n - 1 - j
    rev_tm = pl.BlockSpec((1, tm_rows, c), lambda g, j: (g, rev(j), 0))
    halo = lambda f: pl.BlockSpec((LRU_GROUP, SUBLANES, c), lambda g, j: (g, f(rev(j)), 0))
    xh, hb, hb_fin = pl.pallas_call(
        functools.partial(_lru_bwd_kernel, tc=tc),
        grid=(groups, n),
        in_specs=[
            pl.BlockSpec((LRU_GROUP, tc, c), lambda g, j: (g, rev(j), 0)),
            halo(lambda cj: jnp.maximum(cj * per - 1, 0)),
            halo(lambda cj: jnp.minimum((cj + 1) * per, t // SUBLANES - 1)),
            _resident((CONV_W, c)),
            _resident((1, c)),
        ] + gate_specs,
        out_specs=[rev_tm, rev_tm, state],
        out_shape=[tm_shape, tm_shape, state_shape],
        scratch_shapes=[slabs(X_PAD), carry],
        compiler_params=_cparams(("parallel", "arbitrary"), 56),
        name="lru_bwd",
    )(xb, xb, xb, conv_w, conv_b, wg[1], bg[1], la[1:2], h0[:, :, 1])

    fwd_tm = pl.BlockSpec((1, tm_rows, c), lambda g, j: (g, j, 0))
    chunk = pl.BlockSpec((LRU_GROUP, tc, c), lambda g, j: (g, j, 0))
    rec, hf_fin = pl.pallas_call(
        functools.partial(_lru_fwd_kernel, tc=tc),
        grid=(groups, n),
        in_specs=[fwd_tm, fwd_tm, chunk] + gate_specs,
        out_specs=[chunk, state],
        out_shape=[jax.ShapeDtypeStruct((b, t, c), BF16), state_shape],
        scratch_shapes=[slabs(SCAN_PAD), carry],
        compiler_params=_cparams(("parallel", "arbitrary"), 56),
        name="lru_fwd",
    )(xh, hb, gy, wg[0], bg[0], la[0:1], h0[:, :, 0])
    return rec, jnp.stack([hf_fin.reshape(b, c), hb_fin.reshape(b, c)], axis=1)


def _out_ffn_kernel(x_ref, att_ref, rec_ref, mod_ref, g_ref, wo_ref, wg_ref, wu_ref, wd_ref, *rest, final_norm):
    o_ref = rest[-1]
    mix = jnp.concatenate([att_ref[0], rec_ref[0]], axis=1)
    x1 = x_ref[0] + mod_ref[0, 2:3, :] * jnp.dot(mix, wo_ref[...], preferred_element_type=F32)
    h2 = (_rmsnorm(x1, g_ref[...]) * (1.0 + mod_ref[0, 4:5, :]) + mod_ref[0, 3:4, :]).astype(BF16)
    gate = jnp.dot(h2, wg_ref[...], preferred_element_type=F32)
    up = jnp.dot(h2, wu_ref[...], preferred_element_type=F32)
    act = ((gate * jax.nn.sigmoid(gate)) * up).astype(BF16)
    x2 = x1 + mod_ref[0, 5:6, :] * jnp.dot(act, wd_ref[...], preferred_element_type=F32)
    if final_norm:
        x2 = _rmsnorm(x2, rest[0][...])
    o_ref[0] = x2


def _out_ffn(x, att, rec, mod, g2, w_out, w_gate, w_up, w_down, layer, g_final=None):
    b, t, d = x.shape
    d_ff = w_gate.shape[-1]
    tm = min(512, t)
    shared = mod.shape[0] == 1
    mod_map = (lambda i, j: (0, 0, 0)) if shared else (lambda i, j: (i, 0, 0))
    tok = lambda w: pl.BlockSpec((1, tm, w), lambda i, j: (i, j, 0))
    in_specs = [
        tok(d), tok(ATT_WIDTH), tok(LRU_WIDTH),
        pl.BlockSpec((1, 6, d), mod_map),
        _resident((1, d)),
        _layer_resident((d, d), layer),
        _layer_resident((d, d_ff), layer),
        _layer_resident((d, d_ff), layer),
        _layer_resident((d_ff, d), layer),
    ]
    args = [x, att, rec, mod, g2, w_out, w_gate, w_up, w_down]
    if g_final is not None:
        in_specs.append(_resident((1, d)))
        args.append(g_final)
    return pl.pallas_call(
        functools.partial(_out_ffn_kernel, final_norm=g_final is not None),
        grid=(b, t // tm),
        in_specs=in_specs,
        out_specs=tok(d),
        out_shape=jax.ShapeDtypeStruct((b, t, d), F32),
        compiler_params=_cparams(("parallel", "parallel"), 56),
        name="out_ffn",
    )(*args)


def _block_diag_halves(w):
    per_half = LRU_BLOCKS // 2
    w4 = w.reshape(2, per_half, LRU_BLOCK, LRU_BLOCK)
    eye = jnp.eye(per_half, dtype=w.dtype)
    return jnp.einsum('hnjk,nm->hnjmk', w4, eye).reshape(2, per_half * LRU_BLOCK, per_half * LRU_BLOCK)


def _head_pair_layout(cache):
    b, depth, p = cache.shape[:3]
    return cache.reshape(b, depth, p, HEAD_PAIRS, LANES).transpose(1, 0, 3, 2, 4).astype(BF16)


def kernel(x_prompt, x_sample, cache_k, cache_v, state_lru, c, c_ctx, w_mod, b_mod, norm1, norm2, w_in, w_out, rpb, conv_w, conv_b, lru_a, lru_wr, lru_br, lru_wi, lru_bi, w_gate, w_up, w_down, norm_final):
    depth = w_in.shape[0]
    n_ctx, n_lat = x_prompt.shape[0], x_sample.shape[0]
    d = x_prompt.shape[-1]
    assert n_lat + 1 <= MOD_ROWS

    cvec = jnp.zeros((MOD_ROWS, d), F32).at[:n_lat].set(c).at[n_lat].set(c_ctx)
    mod = _modulation(cvec, w_mod, b_mod).reshape(depth, MOD_ROWS, 6, d)
    t2 = _bias_tables(rpb)
    ck, cv = _head_pair_layout(cache_k), _head_pair_layout(cache_v)

    w_in_b = jnp.concatenate([w_in[..., lo:lo + width] for lo, width in (IN_PROJ_BLOCKS[n] for n in IN_PROJ_ORDER)],
                             axis=-1).astype(BF16)
    w_out_b = w_out.astype(BF16)
    w_gate_b, w_up_b, w_down_b = w_gate.astype(BF16), w_up.astype(BF16), w_down.astype(BF16)
    wg = jnp.stack([
        jnp.stack([jnp.concatenate([_block_diag_halves(lru_wr[l, dr]), _block_diag_halves(lru_wi[l, dr])], axis=-1)
                   for dr in range(2)]) for l in range(depth)]).astype(BF16)
    bg = 0.5 * jnp.stack([lru_br, lru_bi], axis=2)
    conv_w_half, conv_b_half = 0.5 * conv_w, 0.5 * conv_b
    h0_ctx = jnp.zeros((n_ctx, 2, LRU_WIDTH), F32)

    def layer(x, l, mod_l, attend, h0, last, kv_all=None):
        outs = _in_proj(x, mod_l, norm1[l][None], w_in_b, l, kv_all)
        q, k, v, xb, gy = outs[:5]
        att = attend(q, k, v)
        rec, h_fin = _lru(xb, gy, conv_w_half[l], conv_b_half[l][None], wg[l], bg[l], lru_a[l], h0)
        x = _out_ffn(x, att, rec, mod_l, norm2[l][None], w_out_b, w_gate_b, w_up_b, w_down_b, l,
                     g_final=norm_final[None] if last else None)
        return x, outs[5:], h_fin

    xp, xs = x_prompt, x_sample
    kv_all = [jnp.zeros((n_ctx, depth, x_prompt.shape[1], ATT_WIDTH), F32) for _ in range(2)]
    hs_out = []
    for l in range(depth):
        last = l == depth - 1
        xp, kv_all, h_l = layer(xp, l, mod[l, n_lat:n_lat + 1], _ctx_attention, h0_ctx, last, kv_all)
        hs_out.append(h_l)
        attend_lat = functools.partial(_lat_attention, kc=ck[l], vc=cv[l], t2=t2[l])
        xs, _, _ = layer(xs, l, mod[l, :n_lat], attend_lat, state_lru[:, l], last)
    new_k, new_v = (a.reshape(n_ctx, depth, -1, N_ATT_HEADS, HEAD_DIM) for a in kv_all)
    return (xp, xs, new_k, new_v, jnp.stack(hs_out, axis=1))
```

```python
import functools

import jax
import jax.numpy as jnp
from jax import lax
from jax.experimental import pallas as pl
from jax.experimental.pallas import tpu as pltpu

F32 = jnp.float32
BF16 = jnp.bfloat16

D_MODEL = 1024
GRID_W = 64
N_ATT_HEADS = 8
HEAD_DIM = 64
ATT_WIDTH = N_ATT_HEADS * HEAD_DIM
ATT_SCALE = HEAD_DIM ** -0.5
WIN_ROWS = 8
WIN_COLS = 16
LRU_WIDTH = D_MODEL // 2
LRU_BLOCKS = 8
LRU_BLOCK = LRU_WIDTH // LRU_BLOCKS
CONV_W = 4
CONV_LEFT = 2
LRU_C = 8.0
IN_COLS = 3 * ATT_WIDTH + 2 * LRU_WIDTH
EPS = 1e-6
NEG_INF = -1e30
F32_TINY = float(jnp.finfo(jnp.float32).tiny)
LOG2_E = 1.4426950408889634
Q_SCALE = ATT_SCALE * LOG2_E

LANES = 128
SUBLANES = 8
HEAD_PAIRS = ATT_WIDTH // LANES
N_ROW_IDX = 2 * WIN_ROWS - 1
N_COL_IDX = 2 * WIN_COLS - 1
N_BIAS_TILES = N_ROW_IDX - 1
MOD_ROWS = 16
LRU_GROUP = SUBLANES
SCAN_PAD = SUBLANES
X_PAD = 3 * SUBLANES
LRU_TIME_BLOCK = 32
MIB = 1024 * 1024


def _cparams(semantics, vmem_mib):
    return pltpu.CompilerParams(dimension_semantics=semantics, vmem_limit_bytes=vmem_mib * MIB)


def _resident(shape):
    zeros = (0,) * len(shape)
    return pl.BlockSpec(shape, lambda *_: zeros, pipeline_mode=pl.Buffered(1))


def _layer_resident(shape, layer):
    index = (layer,) + (0,) * len(shape)
    return pl.BlockSpec((None,) + tuple(shape), lambda *_: index, pipeline_mode=pl.Buffered(1))


def _rmsnorm(x, g):
    return (x * lax.rsqrt(jnp.mean(x * x, axis=-1, keepdims=True) + EPS)) * g


def _mod_kernel(c_ref, w_ref, b_ref, o_ref):
    cv = c_ref[...]
    s = (cv * jax.nn.sigmoid(cv)).astype(BF16)
    o_ref[0] = jnp.dot(s, w_ref[0].astype(BF16), preferred_element_type=F32) + b_ref[0]


def _modulation(cvec, w_mod, b_mod):
    depth, d, n = w_mod.shape
    tn = n // 4
    return pl.pallas_call(
        _mod_kernel,
        grid=(depth, n // tn),
        in_specs=[
            pl.BlockSpec((MOD_ROWS, d), lambda l, j: (0, 0)),
            pl.BlockSpec((1, d, tn), lambda l, j: (l, 0, j)),
            pl.BlockSpec((1, 1, tn), lambda l, j: (l, 0, j)),
        ],
        out_specs=pl.BlockSpec((1, MOD_ROWS, tn), lambda l, j: (l, 0, j)),
        out_shape=jax.ShapeDtypeStruct((depth, MOD_ROWS, n), F32),
        compiler_params=_cparams(("arbitrary", "arbitrary"), 40),
        name="adaln_mod",
    )(cvec, w_mod, b_mod.reshape(depth, 1, n))


def _bias_table_kernel(rpb_ref, o_ref):
    shape = (GRID_W, LANES)
    c = lax.broadcasted_iota(jnp.int32, shape, 0)
    lane = lax.broadcasted_iota(jnp.int32, shape, 1)
    upper = lane >= GRID_W
    kc = jnp.where(upper, lane - GRID_W, lane)
    cs = jnp.clip(c - WIN_COLS // 2, 0, GRID_W - WIN_COLS)
    ok = (kc >= cs) & (kc < cs + WIN_COLS)
    lo_shift = LANES - (WIN_COLS - 1)
    hi_shift = (lo_shift + GRID_W) % LANES
    for h in range(2):
        for i in range(N_BIAS_TILES):
            lo = pltpu.roll(jnp.broadcast_to(rpb_ref[0, h, i:i + 1, :], shape), lo_shift, 1, stride=1, stride_axis=0)
            hi = pltpu.roll(jnp.broadcast_to(rpb_ref[0, h, i + 1:i + 2, :], shape), hi_shift, 1, stride=1, stride_axis=0)
            o_ref[0, i, h * GRID_W:(h + 1) * GRID_W, :] = jnp.where(ok, jnp.where(upper, hi, lo) * LOG2_E, NEG_INF)


def _bias_tables(rpb):
    depth = rpb.shape[0]
    n = depth * HEAD_PAIRS
    padded = jnp.pad(rpb, ((0, 0), (0, 0), (0, 2 * SUBLANES - N_ROW_IDX), (0, LANES - N_COL_IDX)))
    out = pl.pallas_call(
        _bias_table_kernel,
        grid=(n,),
        in_specs=[pl.BlockSpec((1, 2, 2 * SUBLANES, LANES), lambda i: (i, 0, 0, 0))],
        out_specs=pl.BlockSpec((1, N_BIAS_TILES, LANES, LANES), lambda i: (i, 0, 0, 0)),
        out_shape=jax.ShapeDtypeStruct((n, N_BIAS_TILES, LANES, LANES), F32),
        compiler_params=_cparams(("arbitrary",), 16),
        name="bias_tables",
    )(padded.reshape(n, 2, 2 * SUBLANES, LANES))
    return out.reshape(depth, HEAD_PAIRS, N_BIAS_TILES, LANES, LANES)


def _in_proj_kernel(x_ref, mod_ref, g_ref, w_ref, q_ref, k_ref, v_ref, xb_ref, gy_ref, *kv_f32_refs):
    h = _rmsnorm(x_ref[0], g_ref[...]) * (1.0 + mod_ref[0, 1:2, :]) + mod_ref[0, 0:1, :]
    p = jnp.dot(h.astype(BF16), w_ref[...], preferred_element_type=F32)
    for j in range(HEAD_PAIRS):
        lo = j * LANES
        q_ref[0, j] = (p[:, lo:lo + LANES] * Q_SCALE).astype(BF16)
        k_ref[0, j] = p[:, ATT_WIDTH + lo:ATT_WIDTH + lo + LANES].astype(BF16)
        v_ref[0, j] = p[:, 2 * ATT_WIDTH + lo:2 * ATT_WIDTH + lo + LANES].astype(BF16)
    xb_ref[0] = p[:, 3 * ATT_WIDTH:3 * ATT_WIDTH + LRU_WIDTH]
    gy_ref[0] = jax.nn.gelu(p[:, 3 * ATT_WIDTH + LRU_WIDTH:])
    if kv_f32_refs:
        kf_ref, vf_ref = kv_f32_refs
        kf_ref[0] = p[:, ATT_WIDTH:2 * ATT_WIDTH]
        vf_ref[0] = p[:, 2 * ATT_WIDTH:3 * ATT_WIDTH]


def _in_proj_kv_kernel(x_ref, mod_ref, g_ref, w_ref, k_all_ref, v_all_ref, *out_refs):
    del k_all_ref, v_all_ref
    _in_proj_kernel(x_ref, mod_ref, g_ref, w_ref, *out_refs)


def _in_proj(x, mod, g, w, layer, kv_all=None):
    b, t, d = x.shape
    tm = min(1024, t)
    shared = mod.shape[0] == 1
    mod_map = (lambda i, j: (0, 0, 0)) if shared else (lambda i, j: (i, 0, 0))
    heads_spec = pl.BlockSpec((1, HEAD_PAIRS, tm, LANES), lambda i, j: (i, 0, j, 0))
    heads_shape = jax.ShapeDtypeStruct((b, HEAD_PAIRS, t, LANES), BF16)
    half_spec = pl.BlockSpec((1, tm, LRU_WIDTH), lambda i, j: (i, j, 0))
    half_shape = jax.ShapeDtypeStruct((b, t, LRU_WIDTH), F32)
    in_specs = [
        pl.BlockSpec((1, tm, d), lambda i, j: (i, j, 0)),
        pl.BlockSpec((1, 6, d), mod_map),
        _resident((1, d)),
        _layer_resident((d, IN_COLS), layer),
    ]
    out_specs = [heads_spec] * 3 + [half_spec] * 2
    out_shape = [heads_shape] * 3 + [half_shape] * 2
    args = [x, mod, g, w]
    aliases = {}
    if kv_all is not None:
        layer_spec = pl.BlockSpec((1, None, tm, ATT_WIDTH), lambda i, j: (i, layer, j, 0))
        for arr in kv_all:
            aliases[len(args)] = len(out_specs)
            in_specs.append(pl.BlockSpec(memory_space=pl.ANY))
            out_specs.append(layer_spec)
            out_shape.append(jax.ShapeDtypeStruct(arr.shape, arr.dtype))
            args.append(arr)
    return pl.pallas_call(
        _in_proj_kernel if kv_all is None else _in_proj_kv_kernel,
        grid=(b, t // tm),
        in_specs=in_specs,
        out_specs=out_specs,
        out_shape=out_shape,
        input_output_aliases=aliases,
        compiler_params=_cparams(("parallel", "parallel"), 48),
        name="in_proj",
    )(*args)


def _head_masks(shape):
    lane = lax.broadcasted_iota(jnp.int32, shape, 1)
    return lane < HEAD_DIM, lane >= HEAD_DIM


def _scores(qh, k):
    return lax.dot_general(qh, k, (((1,), (1,)), ((), ())), preferred_element_type=F32)


def _ctx_attn_kernel(q_ref, k_ref, v_ref, o_ref):
    t = q_ref.shape[2]
    own_lanes = ((lax.broadcasted_iota(jnp.int32, (2 * t, LANES), 0) < t)
                 == (lax.broadcasted_iota(jnp.int32, (2 * t, LANES), 1) < HEAD_DIM))
    lo_lanes, _ = _head_masks((t, LANES))
    scores = []
    for hp in range(HEAD_PAIRS):
        q2 = jnp.concatenate([q_ref[0, hp], q_ref[0, hp]], axis=0)
        scores.append(_scores(jnp.where(own_lanes, q2, jnp.zeros_like(q2)), k_ref[0, hp]))
    for hp, s in enumerate(scores):
        e = jnp.exp2(s - jnp.max(s, axis=-1, keepdims=True))
        l = jnp.sum(e, axis=-1, keepdims=True)
        o2 = jnp.dot(e.astype(BF16), v_ref[0, hp], preferred_element_type=F32) / l
        o_ref[0, :, hp * LANES:(hp + 1) * LANES] = jnp.where(lo_lanes, o2[:t], o2[t:]).astype(BF16)


def _ctx_attention(q, k, v):
    b, _, t, _ = q.shape
    spec = pl.BlockSpec((1, HEAD_PAIRS, t, LANES), lambda i: (i, 0, 0, 0))
    return pl.pallas_call(
        _ctx_attn_kernel,
        grid=(b,),
        in_specs=[spec] * 3,
        out_specs=pl.BlockSpec((1, t, ATT_WIDTH), lambda i: (i, 0, 0)),
        out_shape=jax.ShapeDtypeStruct((b, t, ATT_WIDTH), BF16),
        compiler_params=_cparams(("parallel",), 32),
        name="ctx_attention",
    )(q, k, v)


def _lat_attn_kernel(q_ref, k_ref, v_ref, kc_ref, vc_ref, t2_ref, o_ref, *, rows_per_step, n_rows):
    rb = pl.program_id(2)
    kc, vc = kc_ref[0].astype(BF16), vc_ref[0].astype(BF16)
    n_loc = WIN_ROWS * GRID_W
    shape2 = (2 * GRID_W, LANES)
    own_lanes = ((lax.broadcasted_iota(jnp.int32, shape2, 0) < GRID_W)
                 == (lax.broadcasted_iota(jnp.int32, shape2, 1) < HEAD_DIM))
    lo_lanes, _ = _head_masks((GRID_W, LANES))

    def scores(i):
        r = rb * rows_per_step + i
        rs = jnp.clip(r - WIN_ROWS // 2, 0, n_rows - WIN_ROWS)
        off = rs - r + WIN_ROWS - 1
        q = q_ref[0, 0, i * GRID_W:(i + 1) * GRID_W, :]
        q2 = jnp.concatenate([q, q], axis=0)
        q2 = jnp.where(own_lanes, q2, jnp.zeros_like(q2))
        k_start = pl.multiple_of(rs * GRID_W, GRID_W)
        bias = jnp.concatenate([t2_ref[0, off + 2 * j] for j in range(WIN_ROWS // 2)], axis=1)
        s_loc = _scores(q2, k_ref[0, 0, pl.ds(k_start, n_loc), :]) + bias
        s_ctx = _scores(q2, kc)
        return jnp.concatenate([s_loc, s_ctx], axis=1), k_start

    ones_lane = lambda n: (lax.broadcasted_iota(jnp.int32, (n, LANES), 1) == 0).astype(BF16)
    ones_loc, ones_ctx = ones_lane(n_loc), ones_lane(vc.shape[0])
    vc_sum = jnp.concatenate([vc, ones_ctx], axis=1)

    def finish(i, s, m, k_start):
        e = jnp.exp2(s - m).astype(BF16)
        vb_sum = jnp.concatenate([v_ref[0, 0, pl.ds(k_start, n_loc), :], ones_loc], axis=1)
        o_sum = (jnp.dot(e[:, :n_loc], vb_sum, preferred_element_type=F32)
                 + jnp.dot(e[:, n_loc:], vc_sum, preferred_element_type=F32))
        o2 = o_sum[:, :LANES] / o_sum[:, LANES:LANES + 1]
        o_ref[0, i * GRID_W:(i + 1) * GRID_W, :] = jnp.where(lo_lanes, o2[:GRID_W], o2[GRID_W:]).astype(BF16)

    rows = [scores(0), scores(1)] + [None] * rows_per_step
    maxima = [jnp.max(rows[0][0], axis=-1, keepdims=True)] + [None] * rows_per_step
    for i in range(rows_per_step):
        if i + 2 < rows_per_step:
            rows[i + 2] = scores(i + 2)
        if i + 1 < rows_per_step:
            maxima[i + 1] = jnp.max(rows[i + 1][0], axis=-1, keepdims=True)
        finish(i, rows[i][0], maxima[i], rows[i][1])
        rows[i] = maxima[i] = None


def _lat_attention(q, k, v, kc, vc, t2, layer):
    b, _, t, _ = q.shape
    lc = kc.shape[2]
    n_rows = t // GRID_W
    rows_per_step = min(32, n_rows)
    tq = rows_per_step * GRID_W
    full = pl.BlockSpec((1, 1, t, LANES), lambda i, j, r: (i, j, 0, 0))
    ctx = pl.BlockSpec((1, None, lc, LANES), lambda i, j, r: (i, layer, 0, j))
    return pl.pallas_call(
        functools.partial(_lat_attn_kernel, rows_per_step=rows_per_step, n_rows=n_rows),
        grid=(b, HEAD_PAIRS, n_rows // rows_per_step),
        in_specs=[
            pl.BlockSpec((1, 1, tq, LANES), lambda i, j, r: (i, j, r, 0)),
            full, full, ctx, ctx,
            pl.BlockSpec((1, N_BIAS_TILES, LANES, LANES), lambda i, j, r: (j, 0, 0, 0)),
        ],
        out_specs=pl.BlockSpec((1, tq, LANES), lambda i, j, r: (i, r, j)),
        out_shape=jax.ShapeDtypeStruct((b, t, ATT_WIDTH), BF16),
        compiler_params=_cparams(("parallel", "parallel", "arbitrary"), 32),
        name="lat_attention",
    )(q, k, v, kc, vc, t2)


def _log_sigmoid(x):
    return jnp.minimum(x, 0.0) - jnp.log1p(jnp.exp(-jnp.abs(x)))


def _gate_matmuls(xh, wg_ref):
    half = LRU_WIDTH // 2
    xhb = xh.astype(BF16)
    return [jnp.dot(xhb[:, hf * half:(hf + 1) * half], wg_ref[hf], preferred_element_type=F32) for hf in range(2)]


def _lru_coefficients(pre, xh, bg_ref, neg_log_sig4, neg_log2_sig4):
    half = LRU_WIDTH // 2
    t_r = jnp.tanh(jnp.concatenate([pre[0][:, :half], pre[1][:, :half]], axis=1) + bg_ref[0:1, :])
    t_i = jnp.tanh(jnp.concatenate([pre[0][:, half:], pre[1][:, half:]], axis=1) + bg_ref[1:2, :])
    gate = t_r + 1.0
    a = jnp.exp2(gate * neg_log2_sig4)
    z = jnp.tanh(gate * neg_log_sig4) * (1.0 + a * a)
    root = z * lax.rsqrt(jnp.maximum(z, F32_TINY))
    return a, (root * (t_i + 1.0)) * xh


def _pipelined_recurrence(blocks, gate_inputs, xh_block, bg_ref, scales, h, recur):
    m = LRU_TIME_BLOCK * LRU_GROUP
    pre = gate_inputs(blocks[0])
    for pos, blk in enumerate(blocks):
        following = gate_inputs(blocks[pos + 1]) if pos + 1 < len(blocks) else None
        a, b = _lru_coefficients(pre, xh_block(slice(blk * m, (blk + 1) * m)), bg_ref, *scales)
        h = recur(blk, a, b, h)
        pre = following
    return h


def _log_sigmoid_scales(la_ref):
    neg_log_sig4 = (-0.5 * LRU_C) * _log_sigmoid(la_ref[...])
    return neg_log_sig4, neg_log_sig4 * (-LOG2_E)


def _lru_bwd_kernel(xb_ref, xprev_ref, xnext_ref, cw_ref, cb_ref, wg_ref, bg_ref, la_ref, h0_ref,
                    xh_ref, hb_ref, hfin_ref, x_s, carry_s, *, tc):
    j = pl.program_id(1)
    n = pl.num_programs(1)
    cj = n - 1 - j
    x_pitch = tc + X_PAD
    n_slabs = LRU_WIDTH // LANES
    lane_slab = lambda v, c: v[:, c * LANES:(c + 1) * LANES]

    @pl.when(j == 0)
    def _():
        carry_s[...] = h0_ref[0]

    def park(bi, carry):
        base = pl.multiple_of(bi * x_pitch, SUBLANES)
        pieces = ((0, jnp.where(cj > 0, xprev_ref[bi], 0.0)), (SUBLANES, xb_ref[bi]),
                  (SUBLANES + tc, jnp.where(cj < n - 1, xnext_ref[bi], 0.0)))
        for start, v in pieces:
            for c in range(n_slabs):
                x_s[c, pl.ds(base + start, v.shape[0]), :] = lane_slab(v, c)
        return carry

    lax.fori_loop(0, LRU_GROUP, park, 0)

    scales = _log_sigmoid_scales(la_ref)

    def time_rows(t):
        rows = pl.ds(SUBLANES + t, LRU_GROUP, stride=x_pitch)
        return jnp.concatenate([x_s[c, rows, :] for c in range(n_slabs)], axis=1)

    m = LRU_TIME_BLOCK * LRU_GROUP

    def conv_block(blk):
        t0 = blk * LRU_TIME_BLOCK
        xw = jnp.concatenate([time_rows(t0 + k) for k in range(-CONV_LEFT, LRU_TIME_BLOCK + CONV_W - 1 - CONV_LEFT)],
                             axis=0)
        xh = cb_ref[...]
        for tap in range(CONV_W):
            xh = xh + cw_ref[tap:tap + 1, :] * xw[tap * LRU_GROUP:tap * LRU_GROUP + m]
        xh_ref[0, blk * m:(blk + 1) * m, :] = xh
        return _gate_matmuls(xh, wg_ref)

    def recur(blk, a, b, h):
        for i in reversed(range(LRU_TIME_BLOCK)):
            rows = slice(i * LRU_GROUP, (i + 1) * LRU_GROUP)
            h = a[rows] * h + b[rows]
            hb_ref[0, blk * m + i * LRU_GROUP:blk * m + (i + 1) * LRU_GROUP, :] = h
        return h

    blocks = list(reversed(range(tc // LRU_TIME_BLOCK)))
    carry_s[...] = _pipelined_recurrence(blocks, conv_block, lambda rows: xh_ref[0, rows, :], bg_ref, scales,
                                         carry_s[...], recur)

    @pl.when(j == n - 1)
    def _():
        hfin_ref[0] = carry_s[...]


def _lru_fwd_kernel(xh_ref, hb_ref, gy_ref, wg_ref, bg_ref, la_ref, h0_ref, out_ref, hfin_ref,
                    s_s, carry_s, *, tc):
    j = pl.program_id(1)
    n = pl.num_programs(1)
    pitch = tc + SCAN_PAD
    n_slabs = LRU_WIDTH // LANES
    m = LRU_TIME_BLOCK * LRU_GROUP

    @pl.when(j == 0)
    def _():
        carry_s[...] = h0_ref[0]

    scales = _log_sigmoid_scales(la_ref)

    def recur(blk, a, b, h):
        for i in range(LRU_TIME_BLOCK):
            rows = slice(i * LRU_GROUP, (i + 1) * LRU_GROUP)
            h = a[rows] * h + b[rows]
            both = h + hb_ref[0, blk * m + i * LRU_GROUP:blk * m + (i + 1) * LRU_GROUP, :]
            for c in range(n_slabs):
                s_s[c, pl.ds(blk * LRU_TIME_BLOCK + i, LRU_GROUP, stride=pitch), :] = both[:, c * LANES:(c + 1) * LANES]
        return h

    xh_block = lambda rows: xh_ref[0, rows, :]
    carry_s[...] = _pipelined_recurrence(
        list(range(tc // LRU_TIME_BLOCK)),
        lambda blk: _gate_matmuls(xh_block(slice(blk * m, (blk + 1) * m)), wg_ref),
        xh_block, bg_ref, scales, carry_s[...], recur)

    def emit(bi, carry):
        rows = pl.ds(pl.multiple_of(bi * pitch, SUBLANES), tc)
        both = jnp.concatenate([s_s[c, rows, :] for c in range(n_slabs)], axis=1)
        out_ref[bi] = (both * gy_ref[bi]).astype(BF16)
        return carry

    lax.fori_loop(0, LRU_GROUP, emit, 0)

    @pl.when(j == n - 1)
    def _():
        hfin_ref[0] = carry_s[...]


def _lru(xb, gy, conv_w, conv_b, wg, bg, la, h0):
    b, t, c = xb.shape
    assert b % LRU_GROUP == 0
    groups = b // LRU_GROUP
    tc = min(256, t)
    n = t // tc
    per = tc // SUBLANES
    assert tc % (2 * SUBLANES) == 0 and tc % LRU_TIME_BLOCK == 0
    h0 = h0.reshape(groups, LRU_GROUP, 2, c)
    state = pl.BlockSpec((1, LRU_GROUP, c), lambda g, j: (g, 0, 0))
    state_shape = jax.ShapeDtypeStruct((groups, LRU_GROUP, c), F32)
    tm_rows = tc * LRU_GROUP
    tm_shape = jax.ShapeDtypeStruct((groups, t * LRU_GROUP, c), F32)
    slabs = lambda pad: pltpu.VMEM((c // LANES, LRU_GROUP * (tc + pad), LANES), F32)
    carry = pltpu.VMEM((LRU_GROUP, c), F32)
    gate_specs = [_resident((2, c // 2, c)), _resident((2, c)), _resident((1, c)), state]

    rev = lambda j: n - 1 - j
    rev_tm = pl.BlockSpec((1, tm_rows, c), lambda g, j: (g, rev(j), 0))
    halo = lambda f: pl.BlockSpec((LRU_GROUP, SUBLANES, c), lambda g, j: (g, f(rev(j)), 0))
    xh, hb, hb_fin = pl.pallas_call(
        functools.partial(_lru_bwd_kernel, tc=tc),
        grid=(groups, n),
        in_specs=[
            pl.BlockSpec((LRU_GROUP, tc, c), lambda g, j: (g, rev(j), 0)),
            halo(lambda cj: jnp.maximum(cj * per - 1, 0)),
            halo(lambda cj: jnp.minimum((cj + 1) * per, t // SUBLANES - 1)),
            _resident((CONV_W, c)),
            _resident((1, c)),
        ] + gate_specs,
        out_specs=[rev_tm, rev_tm, state],
        out_shape=[tm_shape, tm_shape, state_shape],
        scratch_shapes=[slabs(X_PAD), carry],
        compiler_params=_cparams(("parallel", "arbitrary"), 56),
        name="lru_bwd",
    )(xb, xb, xb, conv_w, conv_b, wg[1], bg[1], la[1:2], h0[:, :, 1])

    fwd_tm = pl.BlockSpec((1, tm_rows, c), lambda g, j: (g, j, 0))
    chunk = pl.BlockSpec((LRU_GROUP, tc, c), lambda g, j: (g, j, 0))
    rec, hf_fin = pl.pallas_call(
        functools.partial(_lru_fwd_kernel, tc=tc),
        grid=(groups, n),
        in_specs=[fwd_tm, fwd_tm, chunk] + gate_specs,
        out_specs=[chunk, state],
        out_shape=[jax.ShapeDtypeStruct((b, t, c), BF16), state_shape],
        scratch_shapes=[slabs(SCAN_PAD), carry],
        compiler_params=_cparams(("parallel", "arbitrary"), 56),
        name="lru_fwd",
    )(xh, hb, gy, wg[0], bg[0], la[0:1], h0[:, :, 0])
    return rec, jnp.stack([hf_fin.reshape(b, c), hb_fin.reshape(b, c)], axis=1)


def _out_ffn_kernel(x_ref, att_ref, rec_ref, mod_ref, g_ref, wo_ref, wg_ref, wu_ref, wd_ref, *rest, final_norm):
    o_ref = rest[-1]
    mix = jnp.concatenate([att_ref[0], rec_ref[0]], axis=1)
    x1 = x_ref[0] + mod_ref[0, 2:3, :] * jnp.dot(mix, wo_ref[...], preferred_element_type=F32)
    h2 = (_rmsnorm(x1, g_ref[...]) * (1.0 + mod_ref[0, 4:5, :]) + mod_ref[0, 3:4, :]).astype(BF16)
    gate = jnp.dot(h2, wg_ref[...], preferred_element_type=F32)
    up = jnp.dot(h2, wu_ref[...], preferred_element_type=F32)
    act = ((gate * jax.nn.sigmoid(gate)) * up).astype(BF16)
    x2 = x1 + mod_ref[0, 5:6, :] * jnp.dot(act, wd_ref[...], preferred_element_type=F32)
    if final_norm:
        x2 = _rmsnorm(x2, rest[0][...])
    o_ref[0] = x2


def _out_ffn(x, att, rec, mod, g2, w_out, w_gate, w_up, w_down, layer, g_final=None):
    b, t, d = x.shape
    d_ff = w_gate.shape[-1]
    tm = min(512, t)
    shared = mod.shape[0] == 1
    mod_map = (lambda i, j: (0, 0, 0)) if shared else (lambda i, j: (i, 0, 0))
    tok = lambda w: pl.BlockSpec((1, tm, w), lambda i, j: (i, j, 0))
    in_specs = [
        tok(d), tok(ATT_WIDTH), tok(LRU_WIDTH),
        pl.BlockSpec((1, 6, d), mod_map),
        _resident((1, d)),
        _layer_resident((d, d), layer),
        _layer_resident((d, d_ff), layer),
        _layer_resident((d, d_ff), layer),
        _layer_resident((d_ff, d), layer),
    ]
    args = [x, att, rec, mod, g2, w_out, w_gate, w_up, w_down]
    if g_final is not None:
        in_specs.append(_resident((1, d)))
        args.append(g_final)
    return pl.pallas_call(
        functools.partial(_out_ffn_kernel, final_norm=g_final is not None),
        grid=(b, t // tm),
        in_specs=in_specs,
        out_specs=tok(d),
        out_shape=jax.ShapeDtypeStruct((b, t, d), F32),
        compiler_params=_cparams(("parallel", "parallel"), 56),
        name="out_ffn",
    )(*args)


def _block_diag_halves(w):
    per_half = LRU_BLOCKS // 2
    w4 = w.reshape(2, per_half, LRU_BLOCK, LRU_BLOCK)
    eye = jnp.eye(per_half, dtype=w.dtype)
    return jnp.einsum('hnjk,nm->hnjmk', w4, eye).reshape(2, per_half * LRU_BLOCK, per_half * LRU_BLOCK)


def kernel(x_prompt, x_sample, cache_k, cache_v, state_lru, c, c_ctx, w_mod, b_mod, norm1, norm2, w_in, w_out, rpb, conv_w, conv_b, lru_a, lru_wr, lru_br, lru_wi, lru_bi, w_gate, w_up, w_down, norm_final):
    depth = w_in.shape[0]
    n_ctx, n_lat = x_prompt.shape[0], x_sample.shape[0]
    d = x_prompt.shape[-1]
    assert n_lat + 1 <= MOD_ROWS

    cvec = jnp.zeros((MOD_ROWS, d), F32).at[:n_lat].set(c).at[n_lat].set(c_ctx)
    mod = _modulation(cvec, w_mod, b_mod).reshape(depth, MOD_ROWS, 6, d)
    t2 = _bias_tables(rpb)
    ck, cv = (a.reshape(*a.shape[:3], ATT_WIDTH) for a in (cache_k, cache_v))

    w_in_b, w_out_b = w_in.astype(BF16), w_out.astype(BF16)
    w_gate_b, w_up_b, w_down_b = w_gate.astype(BF16), w_up.astype(BF16), w_down.astype(BF16)
    wg = jnp.stack([
        jnp.stack([jnp.concatenate([_block_diag_halves(lru_wr[l, dr]), _block_diag_halves(lru_wi[l, dr])], axis=-1)
                   for dr in range(2)]) for l in range(depth)]).astype(BF16)
    bg = 0.5 * jnp.stack([lru_br, lru_bi], axis=2)
    conv_w_half, conv_b_half = 0.5 * conv_w, 0.5 * conv_b
    h0_ctx = jnp.zeros((n_ctx, 2, LRU_WIDTH), F32)

    def layer(x, l, mod_l, attend, h0, last, kv_all=None):
        outs = _in_proj(x, mod_l, norm1[l][None], w_in_b, l, kv_all)
        q, k, v, xb, gy = outs[:5]
        att = attend(q, k, v)
        rec, h_fin = _lru(xb, gy, conv_w_half[l], conv_b_half[l][None], wg[l], bg[l], lru_a[l], h0)
        x = _out_ffn(x, att, rec, mod_l, norm2[l][None], w_out_b, w_gate_b, w_up_b, w_down_b, l,
                     g_final=norm_final[None] if last else None)
        return x, outs[5:], h_fin

    xp, xs = x_prompt, x_sample
    kv_all = [jnp.zeros((n_ctx, depth, x_prompt.shape[1], ATT_WIDTH), F32) for _ in range(2)]
    hs_out = []
    for l in range(depth):
        last = l == depth - 1
        xp, kv_all, h_l = layer(xp, l, mod[l, n_lat:n_lat + 1], _ctx_attention, h0_ctx, last, kv_all)
        hs_out.append(h_l)
        attend_lat = functools.partial(_lat_attention, kc=ck, vc=cv, t2=t2[l], layer=l)
        xs, _, _ = layer(xs, l, mod[l, :n_lat], attend_lat, state_lru[:, l], last)
    new_k, new_v = (a.reshape(n_ctx, depth, -1, N_ATT_HEADS, HEAD_DIM) for a in kv_all)
    return (xp, xs, new_k, new_v, jnp.stack(hs_out, axis=1))
```

```python
import functools

import jax
import jax.numpy as jnp
from jax import lax
from jax.experimental import pallas as pl
from jax.experimental.pallas import tpu as pltpu

F32 = jnp.float32
BF16 = jnp.bfloat16

D_MODEL = 1024
GRID_W = 64
N_ATT_HEADS = 8
HEAD_DIM = 64
ATT_WIDTH = N_ATT_HEADS * HEAD_DIM
ATT_SCALE = HEAD_DIM ** -0.5
WIN_ROWS = 8
WIN_COLS = 16
LRU_WIDTH = D_MODEL // 2
LRU_BLOCKS = 8
LRU_BLOCK = LRU_WIDTH // LRU_BLOCKS
CONV_W = 4
CONV_LEFT = 2
LRU_C = 8.0
IN_COLS = 3 * ATT_WIDTH + 2 * LRU_WIDTH
EPS = 1e-6
NEG_INF = -1e30
F32_TINY = float(jnp.finfo(jnp.float32).tiny)
LOG2_E = 1.4426950408889634
Q_SCALE = ATT_SCALE * LOG2_E

LANES = 128
SUBLANES = 8
HEAD_PAIRS = ATT_WIDTH // LANES
N_ROW_IDX = 2 * WIN_ROWS - 1
N_COL_IDX = 2 * WIN_COLS - 1
N_BIAS_TILES = N_ROW_IDX - 1
MOD_ROWS = 16
LRU_GROUP = SUBLANES
SCAN_PAD = SUBLANES
X_PAD = 3 * SUBLANES
LRU_TIME_BLOCK = 32
MIB = 1024 * 1024


def _cparams(semantics, vmem_mib):
    return pltpu.CompilerParams(dimension_semantics=semantics, vmem_limit_bytes=vmem_mib * MIB)


def _resident(shape):
    zeros = (0,) * len(shape)
    return pl.BlockSpec(shape, lambda *_: zeros, pipeline_mode=pl.Buffered(1))


def _layer_resident(shape, layer):
    index = (layer,) + (0,) * len(shape)
    return pl.BlockSpec((None,) + tuple(shape), lambda *_: index, pipeline_mode=pl.Buffered(1))


def _rmsnorm(x, g):
    return (x * lax.rsqrt(jnp.mean(x * x, axis=-1, keepdims=True) + EPS)) * g


def _mod_kernel(c_ref, w_ref, b_ref, o_ref):
    cv = c_ref[...]
    s = (cv * jax.nn.sigmoid(cv)).astype(BF16)
    o_ref[0] = jnp.dot(s, w_ref[0].astype(BF16), preferred_element_type=F32) + b_ref[0]


def _modulation(cvec, w_mod, b_mod):
    depth, d, n = w_mod.shape
    tn = n // 4
    return pl.pallas_call(
        _mod_kernel,
        grid=(depth, n // tn),
        in_specs=[
            pl.BlockSpec((MOD_ROWS, d), lambda l, j: (0, 0)),
            pl.BlockSpec((1, d, tn), lambda l, j: (l, 0, j)),
            pl.BlockSpec((1, 1, tn), lambda l, j: (l, 0, j)),
        ],
        out_specs=pl.BlockSpec((1, MOD_ROWS, tn), lambda l, j: (l, 0, j)),
        out_shape=jax.ShapeDtypeStruct((depth, MOD_ROWS, n), F32),
        compiler_params=_cparams(("arbitrary", "arbitrary"), 40),
        name="adaln_mod",
    )(cvec, w_mod, b_mod.reshape(depth, 1, n))


def _bias_table_kernel(rpb_ref, o_ref):
    shape = (GRID_W, LANES)
    c = lax.broadcasted_iota(jnp.int32, shape, 0)
    lane = lax.broadcasted_iota(jnp.int32, shape, 1)
    upper = lane >= GRID_W
    kc = jnp.where(upper, lane - GRID_W, lane)
    cs = jnp.clip(c - WIN_COLS // 2, 0, GRID_W - WIN_COLS)
    ok = (kc >= cs) & (kc < cs + WIN_COLS)
    lo_shift = LANES - (WIN_COLS - 1)
    hi_shift = (lo_shift + GRID_W) % LANES
    for h in range(2):
        for i in range(N_BIAS_TILES):
            lo = pltpu.roll(jnp.broadcast_to(rpb_ref[0, h, i:i + 1, :], shape), lo_shift, 1, stride=1, stride_axis=0)
            hi = pltpu.roll(jnp.broadcast_to(rpb_ref[0, h, i + 1:i + 2, :], shape), hi_shift, 1, stride=1, stride_axis=0)
            o_ref[0, i, h * GRID_W:(h + 1) * GRID_W, :] = jnp.where(ok, jnp.where(upper, hi, lo) * LOG2_E, NEG_INF)


def _bias_tables(rpb):
    depth = rpb.shape[0]
    n = depth * HEAD_PAIRS
    padded = jnp.pad(rpb, ((0, 0), (0, 0), (0, 2 * SUBLANES - N_ROW_IDX), (0, LANES - N_COL_IDX)))
    out = pl.pallas_call(
        _bias_table_kernel,
        grid=(n,),
        in_specs=[pl.BlockSpec((1, 2, 2 * SUBLANES, LANES), lambda i: (i, 0, 0, 0))],
        out_specs=pl.BlockSpec((1, N_BIAS_TILES, LANES, LANES), lambda i: (i, 0, 0, 0)),
        out_shape=jax.ShapeDtypeStruct((n, N_BIAS_TILES, LANES, LANES), F32),
        compiler_params=_cparams(("arbitrary",), 16),
        name="bias_tables",
    )(padded.reshape(n, 2, 2 * SUBLANES, LANES))
    return out.reshape(depth, HEAD_PAIRS, N_BIAS_TILES, LANES, LANES)


def _in_proj_kernel(x_ref, mod_ref, g_ref, w_ref, q_ref, k_ref, v_ref, xb_ref, gy_ref, *kv_f32_refs):
    h = _rmsnorm(x_ref[0], g_ref[...]) * (1.0 + mod_ref[0, 1:2, :]) + mod_ref[0, 0:1, :]
    p = jnp.dot(h.astype(BF16), w_ref[...], preferred_element_type=F32)
    for j in range(HEAD_PAIRS):
        lo = j * LANES
        q_ref[0, j] = (p[:, lo:lo + LANES] * Q_SCALE).astype(BF16)
        k_ref[0, j] = p[:, ATT_WIDTH + lo:ATT_WIDTH + lo + LANES].astype(BF16)
        v_ref[0, j] = p[:, 2 * ATT_WIDTH + lo:2 * ATT_WIDTH + lo + LANES].astype(BF16)
    xb_ref[0] = p[:, 3 * ATT_WIDTH:3 * ATT_WIDTH + LRU_WIDTH]
    gy_ref[0] = jax.nn.gelu(p[:, 3 * ATT_WIDTH + LRU_WIDTH:]).astype(BF16)
    if kv_f32_refs:
        kf_ref, vf_ref = kv_f32_refs
        kf_ref[0] = p[:, ATT_WIDTH:2 * ATT_WIDTH]
        vf_ref[0] = p[:, 2 * ATT_WIDTH:3 * ATT_WIDTH]


def _in_proj_kv_kernel(x_ref, mod_ref, g_ref, w_ref, k_all_ref, v_all_ref, *out_refs):
    del k_all_ref, v_all_ref
    _in_proj_kernel(x_ref, mod_ref, g_ref, w_ref, *out_refs)


def _in_proj(x, mod, g, w, layer, kv_all=None):
    b, t, d = x.shape
    tm = min(1024, t)
    shared = mod.shape[0] == 1
    mod_map = (lambda i, j: (0, 0, 0)) if shared else (lambda i, j: (i, 0, 0))
    heads_spec = pl.BlockSpec((1, HEAD_PAIRS, tm, LANES), lambda i, j: (i, 0, j, 0))
    heads_shape = jax.ShapeDtypeStruct((b, HEAD_PAIRS, t, LANES), BF16)
    half_spec = pl.BlockSpec((1, tm, LRU_WIDTH), lambda i, j: (i, j, 0))
    half_shape = jax.ShapeDtypeStruct((b, t, LRU_WIDTH), F32)
    in_specs = [
        pl.BlockSpec((1, tm, d), lambda i, j: (i, j, 0)),
        pl.BlockSpec((1, 6, d), mod_map),
        _resident((1, d)),
        _layer_resident((d, IN_COLS), layer),
    ]
    out_specs = [heads_spec] * 3 + [half_spec] * 2
    out_shape = [heads_shape] * 3 + [half_shape, jax.ShapeDtypeStruct((b, t, LRU_WIDTH), BF16)]
    args = [x, mod, g, w]
    aliases = {}
    if kv_all is not None:
        layer_spec = pl.BlockSpec((1, None, tm, ATT_WIDTH), lambda i, j: (i, layer, j, 0))
        for arr in kv_all:
            aliases[len(args)] = len(out_specs)
            in_specs.append(pl.BlockSpec(memory_space=pl.ANY))
            out_specs.append(layer_spec)
            out_shape.append(jax.ShapeDtypeStruct(arr.shape, arr.dtype))
            args.append(arr)
    return pl.pallas_call(
        _in_proj_kernel if kv_all is None else _in_proj_kv_kernel,
        grid=(b, t // tm),
        in_specs=in_specs,
        out_specs=out_specs,
        out_shape=out_shape,
        input_output_aliases=aliases,
        compiler_params=_cparams(("parallel", "parallel"), 48),
        name="in_proj",
    )(*args)


def _head_masks(shape):
    lane = lax.broadcasted_iota(jnp.int32, shape, 1)
    return lane < HEAD_DIM, lane >= HEAD_DIM


def _scores(qh, k):
    return lax.dot_general(qh, k, (((1,), (1,)), ((), ())), preferred_element_type=F32)


def _ctx_attn_kernel(q_ref, k_ref, v_ref, o_ref):
    t = q_ref.shape[2]
    own_lanes = ((lax.broadcasted_iota(jnp.int32, (2 * t, LANES), 0) < t)
                 == (lax.broadcasted_iota(jnp.int32, (2 * t, LANES), 1) < HEAD_DIM))
    lo_lanes, _ = _head_masks((t, LANES))
    scores = []
    for hp in range(HEAD_PAIRS):
        q2 = jnp.concatenate([q_ref[0, hp], q_ref[0, hp]], axis=0)
        scores.append(_scores(jnp.where(own_lanes, q2, jnp.zeros_like(q2)), k_ref[0, hp]))
    for hp, s in enumerate(scores):
        e = jnp.exp2(s - jnp.max(s, axis=-1, keepdims=True))
        l = jnp.sum(e, axis=-1, keepdims=True)
        o2 = jnp.dot(e.astype(BF16), v_ref[0, hp], preferred_element_type=F32) / l
        o_ref[0, :, hp * LANES:(hp + 1) * LANES] = jnp.where(lo_lanes, o2[:t], o2[t:]).astype(BF16)


def _ctx_attention(q, k, v):
    b, _, t, _ = q.shape
    spec = pl.BlockSpec((1, HEAD_PAIRS, t, LANES), lambda i: (i, 0, 0, 0))
    return pl.pallas_call(
        _ctx_attn_kernel,
        grid=(b,),
        in_specs=[spec] * 3,
        out_specs=pl.BlockSpec((1, t, ATT_WIDTH), lambda i: (i, 0, 0)),
        out_shape=jax.ShapeDtypeStruct((b, t, ATT_WIDTH), BF16),
        compiler_params=_cparams(("parallel",), 32),
        name="ctx_attention",
    )(q, k, v)


def _lat_attn_kernel(q_ref, k_ref, v_ref, kc_ref, vc_ref, t2_ref, o_ref, *, rows_per_step, n_rows):
    rb = pl.program_id(2)
    kc, vc = kc_ref[0, 0], vc_ref[0, 0]
    n_loc = WIN_ROWS * GRID_W
    shape2 = (2 * GRID_W, LANES)
    own_lanes = ((lax.broadcasted_iota(jnp.int32, shape2, 0) < GRID_W)
                 == (lax.broadcasted_iota(jnp.int32, shape2, 1) < HEAD_DIM))
    lo_lanes, _ = _head_masks((GRID_W, LANES))

    def scores(i):
        r = rb * rows_per_step + i
        rs = jnp.clip(r - WIN_ROWS // 2, 0, n_rows - WIN_ROWS)
        off = rs - r + WIN_ROWS - 1
        q = q_ref[0, 0, i * GRID_W:(i + 1) * GRID_W, :]
        q2 = jnp.concatenate([q, q], axis=0)
        q2 = jnp.where(own_lanes, q2, jnp.zeros_like(q2))
        k_start = pl.multiple_of(rs * GRID_W, GRID_W)
        bias = jnp.concatenate([t2_ref[0, off + 2 * j] for j in range(WIN_ROWS // 2)], axis=1)
        s_loc = _scores(q2, k_ref[0, 0, pl.ds(k_start, n_loc), :]) + bias
        s_ctx = _scores(q2, kc)
        return jnp.concatenate([s_loc, s_ctx], axis=1), k_start

    ones_lane = lambda n: (lax.broadcasted_iota(jnp.int32, (n, LANES), 1) == 0).astype(BF16)
    ones_loc, ones_ctx = ones_lane(n_loc), ones_lane(vc.shape[0])
    vc_sum = jnp.concatenate([vc, ones_ctx], axis=1)

    def finish(i, s, m, k_start):
        e = jnp.exp2(s - m).astype(BF16)
        vb_sum = jnp.concatenate([v_ref[0, 0, pl.ds(k_start, n_loc), :], ones_loc], axis=1)
        o_sum = (jnp.dot(e[:, :n_loc], vb_sum, preferred_element_type=F32)
                 + jnp.dot(e[:, n_loc:], vc_sum, preferred_element_type=F32))
        o2 = o_sum[:, :LANES] / o_sum[:, LANES:LANES + 1]
        o_ref[0, i * GRID_W:(i + 1) * GRID_W, :] = jnp.where(lo_lanes, o2[:GRID_W], o2[GRID_W:]).astype(BF16)

    rows = [scores(0), scores(1)] + [None] * rows_per_step
    maxima = [jnp.max(rows[0][0], axis=-1, keepdims=True)] + [None] * rows_per_step
    for i in range(rows_per_step):
        if i + 2 < rows_per_step:
            rows[i + 2] = scores(i + 2)
        if i + 1 < rows_per_step:
            maxima[i + 1] = jnp.max(rows[i + 1][0], axis=-1, keepdims=True)
        finish(i, rows[i][0], maxima[i], rows[i][1])
        rows[i] = maxima[i] = None


def _lat_attention(q, k, v, kc, vc, t2):
    b, _, t, _ = q.shape
    lc = kc.shape[2]
    n_rows = t // GRID_W
    rows_per_step = min(32, n_rows)
    tq = rows_per_step * GRID_W
    full = pl.BlockSpec((1, 1, t, LANES), lambda i, j, r: (i, j, 0, 0))
    ctx = pl.BlockSpec((1, 1, lc, LANES), lambda i, j, r: (i, j, 0, 0))
    return pl.pallas_call(
        functools.partial(_lat_attn_kernel, rows_per_step=rows_per_step, n_rows=n_rows),
        grid=(b, HEAD_PAIRS, n_rows // rows_per_step),
        in_specs=[
            pl.BlockSpec((1, 1, tq, LANES), lambda i, j, r: (i, j, r, 0)),
            full, full, ctx, ctx,
            pl.BlockSpec((1, N_BIAS_TILES, LANES, LANES), lambda i, j, r: (j, 0, 0, 0)),
        ],
        out_specs=pl.BlockSpec((1, tq, LANES), lambda i, j, r: (i, r, j)),
        out_shape=jax.ShapeDtypeStruct((b, t, ATT_WIDTH), BF16),
        compiler_params=_cparams(("parallel", "parallel", "arbitrary"), 32),
        name="lat_attention",
    )(q, k, v, kc, vc, t2)


def _log_sigmoid(x):
    return jnp.minimum(x, 0.0) - jnp.log1p(jnp.exp(-jnp.abs(x)))


def _gate_matmuls(xh, wg_ref):
    half = LRU_WIDTH // 2
    xhb = xh.astype(BF16)
    return [jnp.dot(xhb[:, hf * half:(hf + 1) * half], wg_ref[hf], preferred_element_type=F32) for hf in range(2)]


def _lru_coefficients(pre, xh, bg_ref, neg_log_sig4, neg_log2_sig4):
    half = LRU_WIDTH // 2
    t_r = jnp.tanh(jnp.concatenate([pre[0][:, :half], pre[1][:, :half]], axis=1) + bg_ref[0:1, :])
    t_i = jnp.tanh(jnp.concatenate([pre[0][:, half:], pre[1][:, half:]], axis=1) + bg_ref[1:2, :])
    gate = t_r + 1.0
    a = jnp.exp2(gate * neg_log2_sig4)
    z = jnp.tanh(gate * neg_log_sig4) * (1.0 + a * a)
    root = z * lax.rsqrt(jnp.maximum(z, F32_TINY))
    return a, (root * (t_i + 1.0)) * xh


def _pipelined_recurrence(blocks, gate_inputs, xh_block, bg_ref, scales, h, recur):
    m = LRU_TIME_BLOCK * LRU_GROUP
    pre = gate_inputs(blocks[0])
    for pos, blk in enumerate(blocks):
        following = gate_inputs(blocks[pos + 1]) if pos + 1 < len(blocks) else None
        a, b = _lru_coefficients(pre, xh_block(slice(blk * m, (blk + 1) * m)), bg_ref, *scales)
        h = recur(blk, a, b, h)
        pre = following
    return h


def _log_sigmoid_scales(la_ref):
    neg_log_sig4 = (-0.5 * LRU_C) * _log_sigmoid(la_ref[...])
    return neg_log_sig4, neg_log_sig4 * (-LOG2_E)


def _lru_bwd_kernel(xb_ref, xprev_ref, xnext_ref, cw_ref, cb_ref, wg_ref, bg_ref, la_ref, h0_ref,
                    xh_ref, hb_ref, hfin_ref, x_s, carry_s, *, tc):
    j = pl.program_id(1)
    n = pl.num_programs(1)
    cj = n - 1 - j
    x_pitch = tc + X_PAD
    n_slabs = LRU_WIDTH // LANES
    lane_slab = lambda v, c: v[:, c * LANES:(c + 1) * LANES]

    @pl.when(j == 0)
    def _():
        carry_s[...] = h0_ref[0]

    def park(bi, carry):
        base = pl.multiple_of(bi * x_pitch, SUBLANES)
        pieces = ((0, jnp.where(cj > 0, xprev_ref[bi], 0.0)), (SUBLANES, xb_ref[bi]),
                  (SUBLANES + tc, jnp.where(cj < n - 1, xnext_ref[bi], 0.0)))
        for start, v in pieces:
            for c in range(n_slabs):
                x_s[c, pl.ds(base + start, v.shape[0]), :] = lane_slab(v, c)
        return carry

    lax.fori_loop(0, LRU_GROUP, park, 0)

    scales = _log_sigmoid_scales(la_ref)

    def time_rows(t):
        rows = pl.ds(SUBLANES + t, LRU_GROUP, stride=x_pitch)
        return jnp.concatenate([x_s[c, rows, :] for c in range(n_slabs)], axis=1)

    m = LRU_TIME_BLOCK * LRU_GROUP

    def conv_block(blk):
        t0 = blk * LRU_TIME_BLOCK
        xw = jnp.concatenate([time_rows(t0 + k) for k in range(-CONV_LEFT, LRU_TIME_BLOCK + CONV_W - 1 - CONV_LEFT)],
                             axis=0)
        xh = cb_ref[...]
        for tap in range(CONV_W):
            xh = xh + cw_ref[tap:tap + 1, :] * xw[tap * LRU_GROUP:tap * LRU_GROUP + m]
        xh_ref[0, blk * m:(blk + 1) * m, :] = xh
        return _gate_matmuls(xh, wg_ref)

    def recur(blk, a, b, h):
        for i in reversed(range(LRU_TIME_BLOCK)):
            rows = slice(i * LRU_GROUP, (i + 1) * LRU_GROUP)
            h = a[rows] * h + b[rows]
            hb_ref[0, blk * m + i * LRU_GROUP:blk * m + (i + 1) * LRU_GROUP, :] = h
        return h

    blocks = list(reversed(range(tc // LRU_TIME_BLOCK)))
    carry_s[...] = _pipelined_recurrence(blocks, conv_block, lambda rows: xh_ref[0, rows, :], bg_ref, scales,
                                         carry_s[...], recur)

    @pl.when(j == n - 1)
    def _():
        hfin_ref[0] = carry_s[...]


def _lru_fwd_kernel(xh_ref, hb_ref, gy_ref, wg_ref, bg_ref, la_ref, h0_ref, out_ref, hfin_ref,
                    s_s, carry_s, *, tc):
    j = pl.program_id(1)
    n = pl.num_programs(1)
    pitch = tc + SCAN_PAD
    n_slabs = LRU_WIDTH // LANES
    m = LRU_TIME_BLOCK * LRU_GROUP

    @pl.when(j == 0)
    def _():
        carry_s[...] = h0_ref[0]

    scales = _log_sigmoid_scales(la_ref)

    def recur(blk, a, b, h):
        for i in range(LRU_TIME_BLOCK):
            rows = slice(i * LRU_GROUP, (i + 1) * LRU_GROUP)
            h = a[rows] * h + b[rows]
            both = h + hb_ref[0, blk * m + i * LRU_GROUP:blk * m + (i + 1) * LRU_GROUP, :]
            for c in range(n_slabs):
                s_s[c, pl.ds(blk * LRU_TIME_BLOCK + i, LRU_GROUP, stride=pitch), :] = both[:, c * LANES:(c + 1) * LANES]
        return h

    xh_block = lambda rows: xh_ref[0, rows, :]
    carry_s[...] = _pipelined_recurrence(
        list(range(tc // LRU_TIME_BLOCK)),
        lambda blk: _gate_matmuls(xh_block(slice(blk * m, (blk + 1) * m)), wg_ref),
        xh_block, bg_ref, scales, carry_s[...], recur)

    def emit(bi, carry):
        rows = pl.ds(pl.multiple_of(bi * pitch, SUBLANES), tc)
        both = jnp.concatenate([s_s[c, rows, :] for c in range(n_slabs)], axis=1)
        out_ref[bi] = (both * gy_ref[bi]).astype(BF16)
        return carry

    lax.fori_loop(0, LRU_GROUP, emit, 0)

    @pl.when(j == n - 1)
    def _():
        hfin_ref[0] = carry_s[...]


def _lru(xb, gy, conv_w, conv_b, wg, bg, la, h0):
    b, t, c = xb.shape
    assert b % LRU_GROUP == 0
    groups = b // LRU_GROUP
    tc = min(256, t)
    n = t // tc
    per = tc // SUBLANES
    assert tc % (2 * SUBLANES) == 0 and tc % LRU_TIME_BLOCK == 0
    h0 = h0.reshape(groups, LRU_GROUP, 2, c)
    state = pl.BlockSpec((1, LRU_GROUP, c), lambda g, j: (g, 0, 0))
    state_shape = jax.ShapeDtypeStruct((groups, LRU_GROUP, c), F32)
    tm_rows = tc * LRU_GROUP
    tm_shape = jax.ShapeDtypeStruct((groups, t * LRU_GROUP, c), F32)
    slabs = lambda pad: pltpu.VMEM((c // LANES, LRU_GROUP * (tc + pad), LANES), F32)
    carry = pltpu.VMEM((LRU_GROUP, c), F32)
    gate_specs = [_resident((2, c // 2, c)), _resident((2, c)), _resident((1, c)), state]

    rev = lambda j: n - 1 - j
    rev_tm = pl.BlockSpec((1, tm_rows, c), lambda g, j: (g, rev(j), 0))
    halo = lambda f: pl.BlockSpec((LRU_GROUP, SUBLANES, c), lambda g, j: (g, f(rev(j)), 0))
    xh, hb, hb_fin = pl.pallas_call(
        functools.partial(_lru_bwd_kernel, tc=tc),
        grid=(groups, n),
        in_specs=[
            pl.BlockSpec((LRU_GROUP, tc, c), lambda g, j: (g, rev(j), 0)),
            halo(lambda cj: jnp.maximum(cj * per - 1, 0)),
            halo(lambda cj: jnp.minimum((cj + 1) * per, t // SUBLANES - 1)),
            _resident((CONV_W, c)),
            _resident((1, c)),
        ] + gate_specs,
        out_specs=[rev_tm, rev_tm, state],
        out_shape=[tm_shape, tm_shape, state_shape],
        scratch_shapes=[slabs(X_PAD), carry],
        compiler_params=_cparams(("parallel", "arbitrary"), 56),
        name="lru_bwd",
    )(xb, xb, xb, conv_w, conv_b, wg[1], bg[1], la[1:2], h0[:, :, 1])

    fwd_tm = pl.BlockSpec((1, tm_rows, c), lambda g, j: (g, j, 0))
    chunk = pl.BlockSpec((LRU_GROUP, tc, c), lambda g, j: (g, j, 0))
    rec, hf_fin = pl.pallas_call(
        functools.partial(_lru_fwd_kernel, tc=tc),
        grid=(groups, n),
        in_specs=[fwd_tm, fwd_tm, chunk] + gate_specs,
        out_specs=[chunk, state],
        out_shape=[jax.ShapeDtypeStruct((b, t, c), BF16), state_shape],
        scratch_shapes=[slabs(SCAN_PAD), carry],
        compiler_params=_cparams(("parallel", "arbitrary"), 56),
        name="lru_fwd",
    )(xh, hb, gy, wg[0], bg[0], la[0:1], h0[:, :, 0])
    return rec, jnp.stack([hf_fin.reshape(b, c), hb_fin.reshape(b, c)], axis=1)


def _out_ffn_kernel(x_ref, att_ref, rec_ref, mod_ref, g_ref, wo_ref, wg_ref, wu_ref, wd_ref, *rest, final_norm):
    o_ref = rest[-1]
    mix = jnp.concatenate([att_ref[0], rec_ref[0]], axis=1)
    x1 = x_ref[0] + mod_ref[0, 2:3, :] * jnp.dot(mix, wo_ref[...], preferred_element_type=F32)
    h2 = (_rmsnorm(x1, g_ref[...]) * (1.0 + mod_ref[0, 4:5, :]) + mod_ref[0, 3:4, :]).astype(BF16)
    gate = jnp.dot(h2, wg_ref[...], preferred_element_type=F32)
    up = jnp.dot(h2, wu_ref[...], preferred_element_type=F32)
    act = ((gate * jax.nn.sigmoid(gate)) * up).astype(BF16)
    x2 = x1 + mod_ref[0, 5:6, :] * jnp.dot(act, wd_ref[...], preferred_element_type=F32)
    if final_norm:
        x2 = _rmsnorm(x2, rest[0][...])
    o_ref[0] = x2


def _out_ffn(x, att, rec, mod, g2, w_out, w_gate, w_up, w_down, layer, g_final=None):
    b, t, d = x.shape
    d_ff = w_gate.shape[-1]
    tm = min(512, t)
    shared = mod.shape[0] == 1
    mod_map = (lambda i, j: (0, 0, 0)) if shared else (lambda i, j: (i, 0, 0))
    tok = lambda w: pl.BlockSpec((1, tm, w), lambda i, j: (i, j, 0))
    in_specs = [
        tok(d), tok(ATT_WIDTH), tok(LRU_WIDTH),
        pl.BlockSpec((1, 6, d), mod_map),
        _resident((1, d)),
        _layer_resident((d, d), layer),
        _layer_resident((d, d_ff), layer),
        _layer_resident((d, d_ff), layer),
        _layer_resident((d_ff, d), layer),
    ]
    args = [x, att, rec, mod, g2, w_out, w_gate, w_up, w_down]
    if g_final is not None:
        in_specs.append(_resident((1, d)))
        args.append(g_final)
    return pl.pallas_call(
        functools.partial(_out_ffn_kernel, final_norm=g_final is not None),
        grid=(b, t // tm),
        in_specs=in_specs,
        out_specs=tok(d),
        out_shape=jax.ShapeDtypeStruct((b, t, d), F32),
        compiler_params=_cparams(("parallel", "parallel"), 56),
        name="out_ffn",
    )(*args)


def _block_diag_halves(w):
    per_half = LRU_BLOCKS // 2
    w4 = w.reshape(2, per_half, LRU_BLOCK, LRU_BLOCK)
    eye = jnp.eye(per_half, dtype=w.dtype)
    return jnp.einsum('hnjk,nm->hnjmk', w4, eye).reshape(2, per_half * LRU_BLOCK, per_half * LRU_BLOCK)


def _head_pair_layout(cache):
    b, depth, p = cache.shape[:3]
    return cache.reshape(b, depth, p, HEAD_PAIRS, LANES).transpose(1, 0, 3, 2, 4).astype(BF16)


def kernel(x_prompt, x_sample, cache_k, cache_v, state_lru, c, c_ctx, w_mod, b_mod, norm1, norm2, w_in, w_out, rpb, conv_w, conv_b, lru_a, lru_wr, lru_br, lru_wi, lru_bi, w_gate, w_up, w_down, norm_final):
    depth = w_in.shape[0]
    n_ctx, n_lat = x_prompt.shape[0], x_sample.shape[0]
    d = x_prompt.shape[-1]
    assert n_lat + 1 <= MOD_ROWS

    cvec = jnp.zeros((MOD_ROWS, d), F32).at[:n_lat].set(c).at[n_lat].set(c_ctx)
    mod = _modulation(cvec, w_mod, b_mod).reshape(depth, MOD_ROWS, 6, d)
    t2 = _bias_tables(rpb)
    ck, cv = _head_pair_layout(cache_k), _head_pair_layout(cache_v)

    w_in_b, w_out_b = w_in.astype(BF16), w_out.astype(BF16)
    w_gate_b, w_up_b, w_down_b = w_gate.astype(BF16), w_up.astype(BF16), w_down.astype(BF16)
    wg = jnp.stack([
        jnp.stack([jnp.concatenate([_block_diag_halves(lru_wr[l, dr]), _block_diag_halves(lru_wi[l, dr])], axis=-1)
                   for dr in range(2)]) for l in range(depth)]).astype(BF16)
    bg = 0.5 * jnp.stack([lru_br, lru_bi], axis=2)
    conv_w_half, conv_b_half = 0.5 * conv_w, 0.5 * conv_b
    h0_ctx = jnp.zeros((n_ctx, 2, LRU_WIDTH), F32)

    def layer(x, l, mod_l, attend, h0, last, kv_all=None):
        outs = _in_proj(x, mod_l, norm1[l][None], w_in_b, l, kv_all)
        q, k, v, xb, gy = outs[:5]
        att = attend(q, k, v)
        rec, h_fin = _lru(xb, gy, conv_w_half[l], conv_b_half[l][None], wg[l], bg[l], lru_a[l], h0)
        x = _out_ffn(x, att, rec, mod_l, norm2[l][None], w_out_b, w_gate_b, w_up_b, w_down_b, l,
                     g_final=norm_final[None] if last else None)
        return x, outs[5:], h_fin

    xp, xs = x_prompt, x_sample
    kv_all = [jnp.zeros((n_ctx, depth, x_prompt.shape[1], ATT_WIDTH), F32) for _ in range(2)]
    hs_out = []
    for l in range(depth):
        last = l == depth - 1
        xp, kv_all, h_l = layer(xp, l, mod[l, n_lat:n_lat + 1], _ctx_attention, h0_ctx, last, kv_all)
        hs_out.append(h_l)
        attend_lat = functools.partial(_lat_attention, kc=ck[l], vc=cv[l], t2=t2[l])
        xs, _, _ = layer(xs, l, mod[l, :n_lat], attend_lat, state_lru[:, l], last)
    new_k, new_v = (a.reshape(n_ctx, depth, -1, N_ATT_HEADS, HEAD_DIM) for a in kv_all)
    return (xp, xs, new_k, new_v, jnp.stack(hs_out, axis=1))
```
